```python
import math, functools
import jax, jax.numpy as jnp
from jax import lax
import numpy as np

D_MODEL = 1024
BATCH = 8
SEQ = 2048
DEPTH = 2
DEC_BATCH = 32
DEC_SEQ = 8
PAST_LEN = 16384
PAGE_SIZE = 128

N_A_LAYERS = DEPTH // 2
N_B_LAYERS = DEPTH - N_A_LAYERS
LRU_WIDTH = D_MODEL
N_LRU_BLOCKS = 8
LRU_BLOCK = LRU_WIDTH // N_LRU_BLOCKS
CONV_WIDTH = 4
LRU_C = 8.0
N_HEADS = 8
HEAD_DIM = D_MODEL // (2 * N_HEADS)
KEY_DIM = 2 * HEAD_DIM
VAL_DIM = 2 * HEAD_DIM
ATTN_WIDTH = N_HEADS * VAL_DIM
Q_BLOCK = 128
EPS = 1e-6
NEG_INF = -1e30

kernel_name = "yoco_rglru_diffattn_alibi_step"


def rmsnorm(x, g):
    xf = x.astype(jnp.float32)
    y = xf * lax.rsqrt(jnp.mean(xf * xf, axis=-1, keepdims=True) + EPS)
    return (y * g.astype(jnp.float32)).astype(x.dtype)


def alibi_slopes():
    return jnp.asarray([2.0 ** (-8.0 * (h + 1) / N_HEADS) for h in range(N_HEADS)], dtype=jnp.float32)


def alibi_bias(qpos, kpos):
    dist = (qpos[:, None] - kpos[None, :]).astype(jnp.float32)
    return jnp.where(dist >= 0, -alibi_slopes()[:, None, None] * dist, NEG_INF)


def causal_conv(u, buf, w, b):
    T = u.shape[1]
    full = jnp.concatenate([buf.astype(u.dtype), u], axis=1)
    y = b
    for k in range(CONV_WIDTH):
        y = y + full[:, k:k + T] * w[k]
    return y, full[:, T:]


def rg_lru(x, h0, w_r, b_r, w_i, b_i, lam):
    B, T, _ = x.shape
    xb = x.reshape(B, T, N_LRU_BLOCKS, LRU_BLOCK)
    r = jax.nn.sigmoid(jnp.einsum('btnc,ncd->btnd', xb, w_r).reshape(B, T, LRU_WIDTH) + b_r).astype(jnp.float32)
    i = jax.nn.sigmoid(jnp.einsum('btnc,ncd->btnd', xb, w_i).reshape(B, T, LRU_WIDTH) + b_i).astype(jnp.float32)
    log_a = -LRU_C * r * jax.nn.softplus(-lam.astype(jnp.float32))
    a = jnp.exp(log_a)
    bterm = jnp.sqrt(-jnp.expm1(2.0 * log_a)) * (i * x.astype(jnp.float32))
    bterm = bterm.at[:, 0].add(a[:, 0] * h0.astype(jnp.float32))

    def combine(c1, c2):
        a1, b1 = c1
        a2, b2 = c2
        return a1 * a2, a2 * b1 + b2

    _, h = lax.associative_scan(combine, (a, bterm), axis=1)
    return h.astype(x.dtype), h[:, -1].astype(x.dtype)


def rglru_block(xn, h0, conv0, w_in, conv_w, conv_b, w_r, b_r, w_i, b_i, lam, w_out):
    xbr, gate = jnp.split(xn @ w_in, 2, axis=-1)
    xc, new_conv = causal_conv(xbr, conv0, conv_w, conv_b)
    y, h_last = rg_lru(xc, h0, w_r, b_r, w_i, b_i, lam)
    return (y * jax.nn.silu(gate)) @ w_out, h_last, new_conv


def shared_kv(h, g_kv, w_kv):
    B, T, _ = h.shape
    k, v = jnp.split(rmsnorm(h, g_kv) @ w_kv, 2, axis=-1)
    return k.reshape(B, T, N_HEADS, KEY_DIM), v.reshape(B, T, N_HEADS, VAL_DIM)


def diff_attn_prompt(q1, q2, k, v, lam):
    B, T, H, _ = q1.shape
    qb = Q_BLOCK if T % Q_BLOCK == 0 else T
    nb = T // qb
    scale = HEAD_DIM ** -0.5
    kf = k.astype(jnp.float32)
    k1, k2 = kf[..., :HEAD_DIM], kf[..., HEAD_DIM:]
    vf = v.astype(jnp.float32)
    kpos = jnp.arange(T)

    def to_blocks(q):
        return (q.astype(jnp.float32) * scale).reshape(B, nb, qb, H, HEAD_DIM).transpose(1, 0, 2, 3, 4)

    def one_block(args):
        q1b, q2b, start = args
        bias = alibi_bias(start + jnp.arange(qb), kpos)
        p1 = jax.nn.softmax(jnp.einsum('bqhd,bkhd->bhqk', q1b, k1) + bias, axis=-1)
        p2 = jax.nn.softmax(jnp.einsum('bqhd,bkhd->bhqk', q2b, k2) + bias, axis=-1)
        return jnp.einsum('bhqk,bkhd->bqhd', p1 - lam * p2, vf)

    out = lax.map(one_block, (to_blocks(q1), to_blocks(q2), jnp.arange(nb) * qb))
    return out.transpose(1, 0, 2, 3, 4).reshape(B, T, H, VAL_DIM)


def diff_attn_sample(q1, q2, k_new, v_new, lam, cache_k, cache_v, page_table):
    B, T, H, _ = q1.shape
    n_pages = page_table.shape[1]
    scale = HEAD_DIM ** -0.5
    q1f = q1.astype(jnp.float32) * scale
    q2f = q2.astype(jnp.float32) * scale
    qpos = n_pages * PAGE_SIZE + jnp.arange(T)

    def scores(qf, kpart, kpos):
        return jnp.einsum('bqhd,bkhd->bhqk', qf, kpart) + alibi_bias(qpos, kpos)

    kn = k_new.astype(jnp.float32)
    vn = v_new.astype(jnp.float32)

    def init(qf, kpart):
        s = scores(qf, kpart, qpos)
        m = jnp.max(s, axis=-1)
        p = jnp.exp(s - m[..., None])
        return m, jnp.sum(p, axis=-1), jnp.einsum('bhqk,bkhd->bhqd', p, vn)

    def update(stat, s, vf):
        m, l, acc = stat
        m_new = jnp.maximum(m, jnp.max(s, axis=-1))
        corr = jnp.exp(m - m_new)
        p = jnp.exp(s - m_new[..., None])
        return m_new, l * corr + jnp.sum(p, axis=-1), acc * corr[..., None] + jnp.einsum('bhqk,bkhd->bhqd', p, vf)

    def page_step(carry, xs):
        phys, start = xs
        kp = cache_k[phys].astype(jnp.float32)
        vp = cache_v[phys].astype(jnp.float32)
        kpos = start + jnp.arange(PAGE_SIZE)
        st1 = update(carry[0], scores(q1f, kp[..., :HEAD_DIM], kpos), vp)
        st2 = update(carry[1], scores(q2f, kp[..., HEAD_DIM:], kpos), vp)
        return (st1, st2), None

    carry0 = (init(q1f, kn[..., :HEAD_DIM]), init(q2f, kn[..., HEAD_DIM:]))
    (st1, st2), _ = lax.scan(page_step, carry0, (page_table.T, jnp.arange(n_pages) * PAGE_SIZE))
    out = st1[2] / st1[1][..., None] - lam * (st2[2] / st2[1][..., None])
    return out.transpose(0, 2, 1, 3)


def diff_block(xn, k, v, attend, layer_idx, w_in, lq1, lk1, lq2, lk2, subln, w_out):
    B, T, _ = xn.shape
    q, gate = jnp.split(xn @ w_in, 2, axis=-1)
    q = q.reshape(B, T, N_HEADS, 2, HEAD_DIM)
    lam_init = 0.8 - 0.6 * math.exp(-0.3 * layer_idx)
    lam = (jnp.exp(jnp.sum(lq1.astype(jnp.float32) * lk1.astype(jnp.float32)))
           - jnp.exp(jnp.sum(lq2.astype(jnp.float32) * lk2.astype(jnp.float32))) + lam_init)
    o = attend(q[..., 0, :], q[..., 1, :], k, v, lam)
    o = rmsnorm(o, subln) * (1.0 - lam_init)
    o = o.reshape(B, T, ATTN_WIDTH).astype(xn.dtype) * jax.nn.silu(gate)
    return o @ w_out


def trunk(x, lru_h0, conv0, attend, p):
    h = x
    new_h, new_conv = [], []
    for l in range(N_A_LAYERS):
        xn = rmsnorm(h, p['norm_pre'][l])
        out, hl, cl = rglru_block(xn, lru_h0[l], conv0[l], p['a_w_in'][l], p['a_conv_w'][l], p['a_conv_b'][l],
                                  p['a_w_r'][l], p['a_b_r'][l], p['a_w_i'][l], p['a_b_i'][l],
                                  p['a_lambda'][l], p['a_w_out'][l])
        new_h.append(hl)
        new_conv.append(cl)
        h = h + rmsnorm(out, p['norm_post'][l])
    k, v = shared_kv(h, p['kv_norm'], p['w_kv'])
    for j in range(N_B_LAYERS):
        l = N_A_LAYERS + j
        xn = rmsnorm(h, p['norm_pre'][l])
        out = diff_block(xn, k, v, attend, l, p['b_w_in'][j], p['b_lambda_q1'][j], p['b_lambda_k1'][j],
                         p['b_lambda_q2'][j], p['b_lambda_k2'][j], p['b_subln'][j], p['b_w_out'][j])
        h = h + rmsnorm(out, p['norm_post'][l])
    return h, k, v, jnp.stack(new_h), jnp.stack(new_conv)


def setup_inputs(seed: int = 0) -> dict:
    key = jax.random.key(seed)
    ks = jax.random.split(key, 28)
    f32 = jnp.float32
    n_pages = PAST_LEN // PAGE_SIZE
    n_used = DEC_BATCH * n_pages
    n_pool = n_used + n_used // 4

    def nrm(k, shape, s):
        return jax.random.normal(k, shape, f32) * s

    a_c = jax.random.uniform(ks[16], (N_A_LAYERS, LRU_WIDTH), f32, 0.9, 0.999)
    a_base = a_c ** (1.0 / LRU_C)
    return {
        "x_prompt": nrm(ks[0], (BATCH, SEQ, D_MODEL), 1.0),
        "x_sample": nrm(ks[1], (DEC_BATCH, DEC_SEQ, D_MODEL), 1.0),
        "cache_k": nrm(ks[2], (n_pool, PAGE_SIZE, N_HEADS, KEY_DIM), 1.0),
        "cache_v": nrm(ks[3], (n_pool, PAGE_SIZE, N_HEADS, VAL_DIM), 1.0),
        "page_table": jax.random.permutation(ks[4], n_pool)[:n_used].reshape(DEC_BATCH, n_pages).astype(jnp.int32),
        "state_lru_h": nrm(ks[5], (N_A_LAYERS, DEC_BATCH, LRU_WIDTH), 0.5),
        "state_conv": nrm(ks[6], (N_A_LAYERS, DEC_BATCH, CONV_WIDTH - 1, LRU_WIDTH), 1.0),
        "norm_pre": 1.0 + nrm(ks[7], (DEPTH, D_MODEL), 0.05),
        "norm_post": 1.0 + nrm(ks[8], (DEPTH, D_MODEL), 0.05),
        "a_w_in": nrm(ks[9], (N_A_LAYERS, D_MODEL, 2 * LRU_WIDTH), D_MODEL ** -0.5),
        "a_conv_w": nrm(ks[10], (N_A_LAYERS, CONV_WIDTH, LRU_WIDTH), CONV_WIDTH ** -0.5),
        "a_conv_b": nrm(ks[11], (N_A_LAYERS, LRU_WIDTH), 0.01),
        "a_w_r": nrm(ks[12], (N_A_LAYERS, N_LRU_BLOCKS, LRU_BLOCK, LRU_BLOCK), LRU_BLOCK ** -0.5),
        "a_b_r": nrm(ks[13], (N_A_LAYERS, LRU_WIDTH), 0.01),
        "a_w_i": nrm(ks[14], (N_A_LAYERS, N_LRU_BLOCKS, LRU_BLOCK, LRU_BLOCK), LRU_BLOCK ** -0.5),
        "a_b_i": nrm(ks[15], (N_A_LAYERS, LRU_WIDTH), 0.01),
        "a_lambda": jnp.log(a_base) - jnp.log1p(-a_base),
        "a_w_out": nrm(ks[17], (N_A_LAYERS, LRU_WIDTH, D_MODEL), LRU_WIDTH ** -0.5),
        "kv_norm": 1.0 + nrm(ks[18], (D_MODEL,), 0.05),
        "w_kv": nrm(ks[19], (D_MODEL, N_HEADS * (KEY_DIM + VAL_DIM)), D_MODEL ** -0.5),
        "b_w_in": nrm(ks[20], (N_B_LAYERS, D_MODEL, N_HEADS * 2 * HEAD_DIM + ATTN_WIDTH), D_MODEL ** -0.5),
        "b_lambda_q1": nrm(ks[21], (N_B_LAYERS, HEAD_DIM), 0.1),
        "b_lambda_k1": nrm(ks[22], (N_B_LAYERS, HEAD_DIM), 0.1),
        "b_lambda_q2": nrm(ks[23], (N_B_LAYERS, HEAD_DIM), 0.1),
        "b_lambda_k2": nrm(ks[24], (N_B_LAYERS, HEAD_DIM), 0.1),
        "b_subln": 1.0 + nrm(ks[25], (N_B_LAYERS, VAL_DIM), 0.05),
        "b_w_out": nrm(ks[26], (N_B_LAYERS, ATTN_WIDTH, D_MODEL), ATTN_WIDTH ** -0.5),
    }


def reference(x_prompt, x_sample, cache_k, cache_v, page_table, state_lru_h, state_conv,
              norm_pre, norm_post, a_w_in, a_conv_w, a_conv_b, a_w_r, a_b_r, a_w_i, a_b_i,
              a_lambda, a_w_out, kv_norm, w_kv, b_w_in, b_lambda_q1, b_lambda_k1,
              b_lambda_q2, b_lambda_k2, b_subln, b_w_out):
    p = dict(norm_pre=norm_pre, norm_post=norm_post, a_w_in=a_w_in, a_conv_w=a_conv_w,
             a_conv_b=a_conv_b, a_w_r=a_w_r, a_b_r=a_b_r, a_w_i=a_w_i, a_b_i=a_b_i,
             a_lambda=a_lambda, a_w_out=a_w_out, kv_norm=kv_norm, w_kv=w_kv, b_w_in=b_w_in,
             b_lambda_q1=b_lambda_q1, b_lambda_k1=b_lambda_k1, b_lambda_q2=b_lambda_q2,
             b_lambda_k2=b_lambda_k2, b_subln=b_subln, b_w_out=b_w_out)
    B = x_prompt.shape[0]
    h0_prompt = jnp.zeros((N_A_LAYERS, B, LRU_WIDTH), x_prompt.dtype)
    conv0_prompt = jnp.zeros((N_A_LAYERS, B, CONV_WIDTH - 1, LRU_WIDTH), x_prompt.dtype)
    y_prompt, k_prompt, v_prompt, lru_h_prompt, conv_prompt = trunk(
        x_prompt, h0_prompt, conv0_prompt, diff_attn_prompt, p)
    sample_attend = functools.partial(diff_attn_sample, cache_k=cache_k, cache_v=cache_v, page_table=page_table)
    y_sample, k_sample, v_sample, lru_h_sample, conv_sample = trunk(
        x_sample, state_lru_h, state_conv, sample_attend, p)
    return (y_prompt, y_sample, k_prompt, v_prompt, lru_h_prompt, conv_prompt,
            k_sample, v_sample, lru_h_sample, conv_sample)
```

```python
import functools
import math

import jax
import jax.numpy as jnp
from jax import lax
from jax.experimental import pallas as pl
from jax.experimental.pallas import tpu as pltpu

f32 = jnp.float32
bf16 = jnp.bfloat16

D_MODEL = 1024
LRU_WIDTH = D_MODEL
N_LRU_BLOCKS = 8
LRU_BLOCK = LRU_WIDTH // N_LRU_BLOCKS
CONV_WIDTH = 4
LRU_C = 8.0
N_HEADS = 8
HEAD_DIM = D_MODEL // (2 * N_HEADS)
KEY_DIM = 2 * HEAD_DIM
VAL_DIM = 2 * HEAD_DIM
ATTN_WIDTH = N_HEADS * VAL_DIM
PAGE_SIZE = 128
EPS = 1e-6
NEG_INF = -1e30
LAM_INIT = 0.8 - 0.6 * math.exp(-0.3 * 1)
QK_SCALE = HEAD_DIM ** -0.5

VMEM_LIMIT = 48 * 1024 * 1024

LAYER_A_ROWS = 256
PROJ_ROWS = 256
ATTN_TQ = 256
PAGES_PER_STEP = 4
MAPS = 2
DEC_TOKENS = 8
ROWS_PER_HEAD = MAPS * DEC_TOKENS
DEC_ROWS = N_HEADS * ROWS_PER_HEAD


def _const_spec(shape):
    n = len(shape)
    return pl.BlockSpec(shape, lambda *_: (0,) * n)


def _rms(x, g):
    ms = jnp.mean(x * x, axis=-1, keepdims=True)
    return x * lax.rsqrt(ms + EPS) * g


def _layer_a_kernel(x_ref, h0_ref, conv0_ref, gpre_ref, gpost_ref, winx_ref, wing_ref,
                    cw_ref, cb_ref, wg_ref, br_ref, bi_ref, lam_ref, wout_ref,
                    h1_ref, hlast_ref, convlast_ref,
                    x_s, ext_s, gate_s, y_s, hc_s, *, bn, tt):
    i = pl.program_id(0)
    rows = bn * tt
    hist = (CONV_WIDTH - 1) * bn

    @pl.when(i == 0)
    def _():
        ext_s[0:hist, :] = conv0_ref[...]
        hc_s[...] = h0_ref[...]

    x = pltpu.einshape("btd->tbd", x_ref[...]).reshape(rows, D_MODEL)
    x_s[...] = x
    xn = _rms(x, gpre_ref[...]).astype(bf16)
    ext_s[hist:hist + rows, :] = jnp.dot(xn, winx_ref[...], preferred_element_type=f32)
    gate_s[...] = jnp.dot(xn, wing_ref[...], preferred_element_type=f32)

    for j in range(N_LRU_BLOCKS):
        sl = slice(j * LRU_BLOCK, (j + 1) * LRU_BLOCK)
        xc = cb_ref[:, sl]
        for k in range(CONV_WIDTH):
            xc = xc + ext_s[k * bn:k * bn + rows, sl] * cw_ref[k:k + 1, sl]
        g = jnp.dot(xc.astype(bf16), wg_ref[j], preferred_element_type=f32)
        r = jax.nn.sigmoid(g[:, :LRU_BLOCK] + br_ref[:, sl])
        gi = jax.nn.sigmoid(g[:, LRU_BLOCK:] + bi_ref[:, sl])
        log_a = -LRU_C * r * jax.nn.softplus(-lam_ref[:, sl])
        a = jnp.exp(log_a)
        mult = jnp.sqrt(jnp.tanh(-log_a) * (1.0 + a * a))
        bterm = mult * (gi * xc)
        h = hc_s[:, sl]
        hs = []
        for t in range(tt):
            h = a[t * bn:(t + 1) * bn] * h + bterm[t * bn:(t + 1) * bn]
            hs.append(h)
        hc_s[:, sl] = h
        hseq = jnp.concatenate(hs, axis=0)
        y_s[:, sl] = (hseq * jax.nn.silu(gate_s[:, sl])).astype(bf16)

    out = jnp.dot(y_s[...], wout_ref[...], preferred_element_type=f32)
    h1 = x_s[...] + _rms(out, gpost_ref[...])
    h1_ref[...] = pltpu.einshape("tbd->btd", h1.reshape(tt, bn, D_MODEL))
    hlast_ref[...] = hc_s[...]
    convlast_ref[...] = ext_s[rows:rows + hist, :]
    ext_s[0:hist, :] = ext_s[rows:rows + hist, :]


def _layer_a(x, h0, conv0_tm, p):
    bn, t_len, _ = x.shape
    tt = LAYER_A_ROWS // bn
    rows = bn * tt
    hist = (CONV_WIDTH - 1) * bn
    grid = (t_len // tt,)
    kern = functools.partial(_layer_a_kernel, bn=bn, tt=tt)
    return pl.pallas_call(
        kern,
        grid=grid,
        in_specs=[
            pl.BlockSpec((bn, tt, D_MODEL), lambda i: (0, i, 0)),
            _const_spec((bn, LRU_WIDTH)),
            _const_spec((hist, LRU_WIDTH)),
            _const_spec((1, D_MODEL)),
            _const_spec((1, D_MODEL)),
            _const_spec((D_MODEL, LRU_WIDTH)),
            _const_spec((D_MODEL, LRU_WIDTH)),
            _const_spec((CONV_WIDTH, LRU_WIDTH)),
            _const_spec((1, LRU_WIDTH)),
            _const_spec((N_LRU_BLOCKS, LRU_BLOCK, 2 * LRU_BLOCK)),
            _const_spec((1, LRU_WIDTH)),
            _const_spec((1, LRU_WIDTH)),
            _const_spec((1, LRU_WIDTH)),
            _const_spec((LRU_WIDTH, D_MODEL)),
        ],
        out_specs=[
            pl.BlockSpec((bn, tt, D_MODEL), lambda i: (0, i, 0)),
            _const_spec((bn, LRU_WIDTH)),
            _const_spec((hist, LRU_WIDTH)),
        ],
        out_shape=[
            jax.ShapeDtypeStruct((bn, t_len, D_MODEL), f32),
            jax.ShapeDtypeStruct((bn, LRU_WIDTH), f32),
            jax.ShapeDtypeStruct((hist, LRU_WIDTH), f32),
        ],
        scratch_shapes=[
            pltpu.VMEM((rows, D_MODEL), f32),
            pltpu.VMEM((rows + hist, LRU_WIDTH), f32),
            pltpu.VMEM((rows, LRU_WIDTH), f32),
            pltpu.VMEM((rows, LRU_WIDTH), bf16),
            pltpu.VMEM((bn, LRU_WIDTH), f32),
        ],
        compiler_params=pltpu.CompilerParams(
            dimension_semantics=("arbitrary",), vmem_limit_bytes=VMEM_LIMIT),
        name="layer_a",
    )(x, h0, conv0_tm, p["g_pre0"], p["g_post0"], p["w_in_x"], p["w_in_g"], p["conv_w"],
      p["conv_b"], p["w_gates"], p["b_r"], p["b_i"], p["lam"], p["a_w_out"])


def _proj_kernel(h_ref, gkv_ref, gpre_ref, wkv_ref, wqg_ref,
                 k_ref, v_ref, kb_ref, vb_ref, q_ref, gate_ref):
    h = h_ref[...]
    hn = h * lax.rsqrt(jnp.mean(h * h, axis=-1, keepdims=True) + EPS)
    kv = jnp.dot((hn * gkv_ref[...]).astype(bf16), wkv_ref[...], preferred_element_type=f32)
    k = kv[:, :N_HEADS * KEY_DIM]
    v = kv[:, N_HEADS * KEY_DIM:]
    n_tok = h.shape[0]
    for hd in range(N_HEADS):
        k_ref[pl.ds(hd, n_tok, stride=N_HEADS), :] = k[:, hd * KEY_DIM:(hd + 1) * KEY_DIM]
        v_ref[pl.ds(hd, n_tok, stride=N_HEADS), :] = v[:, hd * VAL_DIM:(hd + 1) * VAL_DIM]
    kb_ref[...] = k.astype(bf16)
    vb_ref[...] = v.astype(bf16)
    qg = jnp.dot((hn * gpre_ref[...]).astype(bf16), wqg_ref[...], preferred_element_type=f32)
    q_ref[...] = (qg[:, :N_HEADS * KEY_DIM] * QK_SCALE).astype(bf16)
    gate_ref[...] = qg[:, N_HEADS * KEY_DIM:]


def _proj(h2d, p):
    n_rows = h2d.shape[0]
    tm = PROJ_ROWS
    row_spec = pl.BlockSpec((tm, D_MODEL), lambda i: (i, 0))
    head_row_spec = pl.BlockSpec((tm * N_HEADS, KEY_DIM), lambda i: (i, 0))
    return pl.pallas_call(
        _proj_kernel,
        grid=(n_rows // tm,),
        in_specs=[
            row_spec,
            _const_spec((1, D_MODEL)),
            _const_spec((1, D_MODEL)),
            _const_spec((D_MODEL, 2 * D_MODEL)),
            _const_spec((D_MODEL, 2 * D_MODEL)),
        ],
        out_specs=[head_row_spec] * 2 + [row_spec] * 4,
        out_shape=[
            jax.ShapeDtypeStruct((n_rows * N_HEADS, KEY_DIM), f32),
            jax.ShapeDtypeStruct((n_rows * N_HEADS, VAL_DIM), f32),
            jax.ShapeDtypeStruct((n_rows, D_MODEL), bf16),
            jax.ShapeDtypeStruct((n_rows, D_MODEL), bf16),
            jax.ShapeDtypeStruct((n_rows, D_MODEL), bf16),
            jax.ShapeDtypeStruct((n_rows, D_MODEL), f32),
        ],
        compiler_params=pltpu.CompilerParams(
            dimension_semantics=("arbitrary",), vmem_limit_bytes=VMEM_LIMIT),
        name="proj",
    )(h2d, p["g_kv"], p["g_pre1"], p["w_kv"], p["b_w_in"])


def _lambda_full(lamv_ref):
    lv = lamv_ref[...]
    e1 = jnp.exp(jnp.sum(lv[0:1] * lv[1:2], axis=-1, keepdims=True))
    e2 = jnp.exp(jnp.sum(lv[2:3] * lv[3:4], axis=-1, keepdims=True))
    return e1 - e2 + LAM_INIT


def _subln(o, subln):
    return (o * lax.rsqrt(jnp.mean(o * o, axis=-1, keepdims=True) + EPS) * subln) * (1.0 - LAM_INIT)


def _attn_prompt_kernel(slopes_ref, q_ref, k_ref, v_ref, lamv_ref, subln_ref, o_ref,
                        m_s, l_s, acc_s, *, tq):
    head = pl.program_id(1)
    qi = pl.program_id(2)
    slope = slopes_ref[head]
    rows = MAPS * tq

    q = q_ref[...]
    lane = lax.broadcasted_iota(jnp.int32, (tq, KEY_DIM), 1)
    zero = jnp.zeros_like(q)
    q2 = jnp.concatenate([jnp.where(lane < HEAD_DIM, q, zero),
                          jnp.where(lane >= HEAD_DIM, q, zero)], axis=0)

    ri = lax.broadcasted_iota(jnp.int32, (rows, tq), 0)
    ri = jnp.where(ri >= tq, ri - tq, ri)
    ci = lax.broadcasted_iota(jnp.int32, (rows, tq), 1)
    rel = ri - ci
    base = (-slope) * rel.astype(f32)

    m_s[...] = jnp.full((rows, 1), NEG_INF, f32)
    l_s[...] = jnp.zeros((rows, 1), f32)
    acc_s[...] = jnp.zeros((rows, VAL_DIM), f32)

    def step(j, masked):
        start = pl.multiple_of(j * tq, tq)
        k = k_ref[pl.ds(start, tq), :]
        v = v_ref[pl.ds(start, tq), :]
        s = lax.dot_general(q2, k, (((1,), (1,)), ((), ())), preferred_element_type=f32) + base
        if masked:
            s = jnp.where(rel >= 0, s, NEG_INF)
        c = (-slope) * ((qi - j) * tq).astype(f32)
        m_old = m_s[...]
        m_new = jnp.maximum(m_old, jnp.max(s, axis=-1, keepdims=True) + c)
        p = jnp.exp(s - (m_new - c))
        corr = jnp.exp(m_old - m_new)
        l_s[...] = l_s[...] * corr + jnp.sum(p, axis=-1, keepdims=True)
        acc_s[...] = acc_s[...] * corr + jnp.dot(p.astype(bf16), v, preferred_element_type=f32)
        m_s[...] = m_new

    def body(j, carry):
        step(j, False)
        return carry

    lax.fori_loop(0, qi, body, 0)
    step(qi, True)

    lam = _lambda_full(lamv_ref)
    acc = acc_s[...]
    l = l_s[...]
    o = acc[:tq] / l[:tq] - lam * (acc[tq:] / l[tq:])
    o_ref[...] = _subln(o, subln_ref[...])


def _attn_prompt(q, kb, vb, p):
    b, t_len, _ = q.shape
    tq = ATTN_TQ
    kern = functools.partial(_attn_prompt_kernel, tq=tq)
    return pl.pallas_call(
        kern,
        grid=(b, N_HEADS, t_len // tq),
        in_specs=[
            pl.BlockSpec(memory_space=pltpu.SMEM),
            pl.BlockSpec((None, tq, KEY_DIM), lambda bi, h, qi: (bi, qi, h)),
            pl.BlockSpec((None, t_len, KEY_DIM), lambda bi, h, qi: (bi, 0, h)),
            pl.BlockSpec((None, t_len, VAL_DIM), lambda bi, h, qi: (bi, 0, h)),
            _const_spec((4, HEAD_DIM)),
            _const_spec((1, VAL_DIM)),
        ],
        out_specs=pl.BlockSpec((None, tq, VAL_DIM), lambda bi, h, qi: (bi, qi, h)),
        out_shape=jax.ShapeDtypeStruct((b, t_len, ATTN_WIDTH), f32),
        scratch_shapes=[
            pltpu.VMEM((MAPS * tq, 1), f32),
            pltpu.VMEM((MAPS * tq, 1), f32),
            pltpu.VMEM((MAPS * tq, VAL_DIM), f32),
        ],
        compiler_params=pltpu.CompilerParams(
            dimension_semantics=("arbitrary", "arbitrary", "arbitrary"),
            vmem_limit_bytes=VMEM_LIMIT),
        name="attn_prompt",
    )(p["slopes"], q, kb, vb, p["lamv"], p["subln"])


def _attn_decode_kernel(pt_ref, qrow_ref, qbd_ref, knew_ref, vnew_ref, slope_ref, pagebias_ref,
                        lamv_ref, subln_ref, *rest, past_len):
    npg = PAGES_PER_STEP
    k_refs = rest[:npg]
    v_refs = rest[npg:2 * npg]
    o_ref = rest[2 * npg]
    m_s, l_s, acc_s, kpad_s, vpad_s = rest[2 * npg + 1:]
    i = pl.program_id(1)

    slope = slope_ref[...]
    row = lax.broadcasted_iota(jnp.int32, (DEC_ROWS, 1), 0)
    tok = jnp.bitwise_and(row, DEC_TOKENS - 1)

    def update(s, c, pv_fn):
        m_old = m_s[...]
        m_new = jnp.maximum(m_old, jnp.max(s, axis=-1, keepdims=True) + c)
        p32 = jnp.exp(s - (m_new - c))
        corr = jnp.exp(m_old - m_new)
        l_s[...] = l_s[...] * corr + jnp.sum(p32, axis=-1, keepdims=True)
        acc_s[...] = acc_s[...] * corr + pv_fn(p32.astype(bf16))
        m_s[...] = m_new

    @pl.when(i == 0)
    def _():
        m_s[...] = jnp.full((DEC_ROWS, 1), NEG_INF, f32)
        l_s[...] = jnp.zeros((DEC_ROWS, 1), f32)
        acc_s[...] = jnp.zeros((DEC_ROWS, VAL_DIM), f32)
        kpad_s[...] = jnp.zeros(kpad_s.shape, bf16)
        vpad_s[...] = jnp.zeros(vpad_s.shape, bf16)
        kpad_s[0:2 * DEC_TOKENS, :] = jnp.concatenate(
            [knew_ref[...], jnp.zeros_like(knew_ref[...])], axis=0)
        vpad_s[0:2 * DEC_TOKENS, :] = jnp.concatenate(
            [vnew_ref[...], jnp.zeros_like(vnew_ref[...])], axis=0)
        key = lax.broadcasted_iota(jnp.int32, (1, PAGE_SIZE), 1)
        dist = tok - key
        s = lax.dot_general(qbd_ref[...], kpad_s[...], (((1,), (1,)), ((), ())),
                            preferred_element_type=f32)
        s = jnp.where(dist >= 0, s - slope * dist.astype(f32), NEG_INF)

        def pv_new(p):
            full = jnp.dot(p, vpad_s[...], preferred_element_type=f32)
            return jnp.concatenate(
                [full[h * ROWS_PER_HEAD:(h + 1) * ROWS_PER_HEAD, h * VAL_DIM:(h + 1) * VAL_DIM]
                 for h in range(N_HEADS)], axis=0)

        update(s, jnp.zeros((DEC_ROWS, 1), f32), pv_new)

    qrow = qrow_ref[...]
    for n in range(npg):
        k = k_refs[n][...].reshape(PAGE_SIZE * N_HEADS, KEY_DIM).astype(bf16)
        v = v_refs[n][...].reshape(PAGE_SIZE * N_HEADS, VAL_DIM).astype(bf16)
        s = lax.dot_general(qrow, k, (((1,), (1,)), ((), ())), preferred_element_type=f32)
        s = s + pagebias_ref[...]
        page_start = (i * npg + n) * PAGE_SIZE
        c = -slope * (past_len + tok - page_start).astype(f32)
        update(s, c, lambda p, v=v: jnp.dot(p, v, preferred_element_type=f32))

    @pl.when(i == pl.num_programs(1) - 1)
    def _():
        lam = _lambda_full(lamv_ref)
        accn = acc_s[...] / l_s[...]
        for h in range(N_HEADS):
            r0 = h * ROWS_PER_HEAD
            o = accn[r0:r0 + DEC_TOKENS] - lam * accn[r0 + DEC_TOKENS:r0 + 2 * DEC_TOKENS]
            o_ref[:, h * VAL_DIM:(h + 1) * VAL_DIM] = _subln(o, subln_ref[...])


def _attn_decode(q, kb_new, vb_new, cache_k, cache_v, page_table, p):
    b, n_tok, _ = q.shape
    assert n_tok == DEC_TOKENS
    n_pages = page_table.shape[1]
    npg = PAGES_PER_STEP

    q5 = q.reshape(b, n_tok, N_HEADS, MAPS, HEAD_DIM).transpose(0, 2, 3, 1, 4)
    eye_m = jnp.eye(MAPS, dtype=q.dtype)
    eye_h = jnp.eye(N_HEADS, dtype=q.dtype)
    qrow = (q5[:, :, :, :, None, :] * eye_m[None, None, :, None, :, None])
    qbd = qrow[:, :, :, :, None, :, :] * eye_h[None, :, None, None, :, None, None]
    qrow = qrow.reshape(b, DEC_ROWS, KEY_DIM)
    qbd = qbd.reshape(b, DEC_ROWS, N_HEADS * KEY_DIM)

    def page_spec(n):
        return pl.BlockSpec((None, PAGE_SIZE, N_HEADS, KEY_DIM),
                            lambda bi, i, pt: (pt[bi, i * npg + n], 0, 0, 0))

    def seq_spec(shape):
        return pl.BlockSpec((None,) + shape, lambda bi, i, pt: (bi, 0, 0))

    def const_spec(shape):
        return pl.BlockSpec(shape, lambda bi, i, pt: (0, 0))

    grid_spec = pltpu.PrefetchScalarGridSpec(
        num_scalar_prefetch=1,
        grid=(b, n_pages // npg),
        in_specs=[
            seq_spec((DEC_ROWS, KEY_DIM)),
            seq_spec((DEC_ROWS, N_HEADS * KEY_DIM)),
            seq_spec((n_tok, N_HEADS * KEY_DIM)),
            seq_spec((n_tok, N_HEADS * VAL_DIM)),
            const_spec((DEC_ROWS, 1)),
            const_spec((DEC_ROWS, PAGE_SIZE * N_HEADS)),
            const_spec((4, HEAD_DIM)),
            const_spec((1, VAL_DIM)),
        ] + [page_spec(n) for n in range(npg)] + [page_spec(n) for n in range(npg)],
        out_specs=seq_spec((n_tok, ATTN_WIDTH)),
        scratch_shapes=[
            pltpu.VMEM((DEC_ROWS, 1), f32),
            pltpu.VMEM((DEC_ROWS, 1), f32),
            pltpu.VMEM((DEC_ROWS, VAL_DIM), f32),
            pltpu.VMEM((PAGE_SIZE, N_HEADS * KEY_DIM), bf16),
            pltpu.VMEM((PAGE_SIZE, N_HEADS * VAL_DIM), bf16),
        ],
    )
    kern = functools.partial(_attn_decode_kernel, past_len=n_pages * PAGE_SIZE)
    return pl.pallas_call(
        kern,
        grid_spec=grid_spec,
        out_shape=jax.ShapeDtypeStruct((b, n_tok, ATTN_WIDTH), f32),
        compiler_params=pltpu.CompilerParams(
            dimension_semantics=("arbitrary", "arbitrary"), vmem_limit_bytes=VMEM_LIMIT),
        name="attn_decode",
    )(page_table, qrow, qbd, kb_new, vb_new, p["slope_rows"], p["page_bias"], p["lamv"],
      p["subln"], *([cache_k] * npg), *([cache_v] * npg))


def _tail_kernel(o_ref, gate_ref, h_ref, wout_ref, gpost_ref, y_ref):
    z = (o_ref[...] * jax.nn.silu(gate_ref[...])).astype(bf16)
    out = jnp.dot(z, wout_ref[...], preferred_element_type=f32)
    y_ref[...] = h_ref[...] + _rms(out, gpost_ref[...])


def _tail(o2d, gate2d, h2d, p):
    n_rows = o2d.shape[0]
    tm = PROJ_ROWS
    row_spec = pl.BlockSpec((tm, D_MODEL), lambda i: (i, 0))
    return pl.pallas_call(
        _tail_kernel,
        grid=(n_rows // tm,),
        in_specs=[row_spec, row_spec, row_spec,
                  _const_spec((ATTN_WIDTH, D_MODEL)), _const_spec((1, D_MODEL))],
        out_specs=row_spec,
        out_shape=jax.ShapeDtypeStruct((n_rows, D_MODEL), f32),
        compiler_params=pltpu.CompilerParams(
            dimension_semantics=("arbitrary",), vmem_limit_bytes=VMEM_LIMIT),
        name="tail",
    )(o2d, gate2d, h2d, p["b_w_out"], p["g_post1"])


def _trunk(x, h0, conv0, p, attend):
    b, t_len, _ = x.shape
    conv0_tm = conv0.transpose(1, 0, 2).reshape((CONV_WIDTH - 1) * b, LRU_WIDTH)
    h1, h_last, conv_last = _layer_a(x, h0, conv0_tm, p)
    h2d = h1.reshape(b * t_len, D_MODEL)
    k, v, kb, vb, q, gate = _proj(h2d, p)
    o = attend(q.reshape(b, t_len, -1), kb.reshape(b, t_len, -1), vb.reshape(b, t_len, -1))
    y = _tail(o.reshape(b * t_len, ATTN_WIDTH), gate, h2d, p)
    new_conv = conv_last.reshape(CONV_WIDTH - 1, b, LRU_WIDTH).transpose(1, 0, 2)
    return (y.reshape(b, t_len, D_MODEL), k.reshape(b, t_len, N_HEADS, KEY_DIM),
            v.reshape(b, t_len, N_HEADS, VAL_DIM), h_last[None], new_conv[None])


def _page_bias(slopes):
    row_head = jnp.arange(DEC_ROWS) // ROWS_PER_HEAD
    col = jnp.arange(PAGE_SIZE * N_HEADS)
    col_pos = (col // N_HEADS).astype(f32)
    col_head = col % N_HEADS
    own = row_head[:, None] == col_head[None, :]
    return jnp.where(own, slopes[row_head][:, None] * col_pos[None, :], NEG_INF).astype(f32)


def kernel(x_prompt, x_sample, cache_k, cache_v, page_table, state_lru_h, state_conv, norm_pre, norm_post, a_w_in, a_conv_w, a_conv_b, a_w_r, a_b_r, a_w_i, a_b_i, a_lambda, a_w_out, kv_norm, w_kv, b_w_in, b_lambda_q1, b_lambda_k1, b_lambda_q2, b_lambda_k2, b_subln, b_w_out):
    slopes = jnp.asarray([2.0 ** (-8.0 * (h + 1) / N_HEADS) for h in range(N_HEADS)], dtype=f32)
    p = dict(
        g_pre0=norm_pre[0:1], g_post0=norm_post[0:1], g_pre1=norm_pre[1:2], g_post1=norm_post[1:2],
        w_in_x=a_w_in[0, :, :LRU_WIDTH].astype(bf16), w_in_g=a_w_in[0, :, LRU_WIDTH:].astype(bf16),
        conv_w=a_conv_w[0], conv_b=a_conv_b[0:1],
        w_gates=jnp.concatenate([a_w_r[0], a_w_i[0]], axis=-1).astype(bf16),
        b_r=a_b_r[0:1], b_i=a_b_i[0:1], lam=a_lambda[0:1], a_w_out=a_w_out[0].astype(bf16),
        g_kv=kv_norm[None, :], w_kv=w_kv.astype(bf16), b_w_in=b_w_in[0].astype(bf16),
        lamv=jnp.concatenate([b_lambda_q1, b_lambda_k1, b_lambda_q2, b_lambda_k2], axis=0),
        subln=b_subln[0:1], b_w_out=b_w_out[0].astype(bf16),
        slopes=slopes, slope_rows=jnp.repeat(slopes, ROWS_PER_HEAD)[:, None],
        page_bias=_page_bias(slopes),
    )

    def attend_prompt(q, kb, vb):
        return _attn_prompt(q, kb, vb, p)

    def attend_sample(q, kb, vb):
        return _attn_decode(q, kb, vb, cache_k, cache_v, page_table, p)

    bp = x_prompt.shape[0]
    h0_prompt = jnp.zeros((bp, LRU_WIDTH), f32)
    conv0_prompt = jnp.zeros((bp, CONV_WIDTH - 1, LRU_WIDTH), f32)
    y_p, k_p, v_p, lru_p, conv_p = _trunk(x_prompt, h0_prompt, conv0_prompt, p, attend_prompt)
    y_s, k_s, v_s, lru_s, conv_s = _trunk(x_sample, state_lru_h[0], state_conv[0], p, attend_sample)
    return (y_p, y_s, k_p, v_p, lru_p, conv_p, k_s, v_s, lru_s, conv_s)
```

```python
import functools
import math

import jax
import jax.numpy as jnp
from jax import lax
from jax.experimental import pallas as pl
from jax.experimental.pallas import tpu as pltpu

f32 = jnp.float32
bf16 = jnp.bfloat16

D_MODEL = 1024
LRU_WIDTH = D_MODEL
N_LRU_BLOCKS = 8
LRU_BLOCK = LRU_WIDTH // N_LRU_BLOCKS
CONV_WIDTH = 4
LRU_C = 8.0
N_HEADS = 8
HEAD_DIM = D_MODEL // (2 * N_HEADS)
KEY_DIM = 2 * HEAD_DIM
VAL_DIM = 2 * HEAD_DIM
ATTN_WIDTH = N_HEADS * VAL_DIM
PAGE_SIZE = 128
EPS = 1e-6
NEG_INF = -1e30
LAM_INIT = 0.8 - 0.6 * math.exp(-0.3 * 1)
QK_SCALE = HEAD_DIM ** -0.5

VMEM_LIMIT = 48 * 1024 * 1024

LAYER_A_ROWS = 256
PROJ_ROWS = 256
ATTN_TQ = 512
PAGES_PER_STEP = 4
MAPS = 2
DEC_TOKENS = 8
COLS_PER_HEAD = MAPS * DEC_TOKENS
DEC_COLS = N_HEADS * COLS_PER_HEAD
PAGE_ROWS = PAGE_SIZE * N_HEADS
NEW_ROWS = DEC_TOKENS * N_HEADS


def _const_spec(shape):
    n = len(shape)
    return pl.BlockSpec(shape, lambda *_: (0,) * n)


def _rms(x, g):
    ms = jnp.mean(x * x, axis=-1, keepdims=True)
    return x * lax.rsqrt(ms + EPS) * g


def _layer_a_kernel(x_ref, h0_ref, conv0_ref, gpre_ref, gpost_ref, winx_ref, wing_ref,
                    cw_ref, cb_ref, wg_ref, br_ref, bi_ref, lam_ref, wout_ref,
                    h1_ref, hlast_ref, convlast_ref,
                    x_s, ext_s, gate_s, y_s, hc_s, *, bn, tt):
    i = pl.program_id(0)
    rows = bn * tt
    hist = (CONV_WIDTH - 1) * bn

    @pl.when(i == 0)
    def _():
        ext_s[0:hist, :] = conv0_ref[...]
        hc_s[...] = h0_ref[...]

    x = jnp.swapaxes(x_ref[...], 0, 1).reshape(rows, D_MODEL)
    x_s[...] = x
    xn = _rms(x, gpre_ref[...]).astype(bf16)
    ext_s[hist:hist + rows, :] = jnp.dot(xn, winx_ref[...], preferred_element_type=f32)
    gate_s[...] = jnp.dot(xn, wing_ref[...], preferred_element_type=f32)

    for j in range(N_LRU_BLOCKS):
        sl = slice(j * LRU_BLOCK, (j + 1) * LRU_BLOCK)
        xc = cb_ref[:, sl]
        for k in range(CONV_WIDTH):
            xc = xc + ext_s[k * bn:k * bn + rows, sl] * cw_ref[k:k + 1, sl]
        g = jnp.dot(xc.astype(bf16), wg_ref[j], preferred_element_type=f32)
        r = jax.nn.sigmoid(g[:, :LRU_BLOCK] + br_ref[:, sl])
        gi = jax.nn.sigmoid(g[:, LRU_BLOCK:] + bi_ref[:, sl])
        log_a = -LRU_C * r * jax.nn.softplus(-lam_ref[:, sl])
        a = jnp.exp(log_a)
        mult = jnp.sqrt(jnp.tanh(-log_a) * (1.0 + a * a))
        bterm = mult * (gi * xc)
        h = hc_s[:, sl]
        hs = []
        for t in range(tt):
            h = a[t * bn:(t + 1) * bn] * h + bterm[t * bn:(t + 1) * bn]
            hs.append(h)
        hc_s[:, sl] = h
        hseq = jnp.concatenate(hs, axis=0)
        y_s[:, sl] = (hseq * jax.nn.silu(gate_s[:, sl])).astype(bf16)

    out = jnp.dot(y_s[...], wout_ref[...], preferred_element_type=f32)
    h1 = x_s[...] + _rms(out, gpost_ref[...])
    h1_ref[...] = jnp.swapaxes(h1.reshape(tt, bn, D_MODEL), 0, 1)
    hlast_ref[...] = hc_s[...]
    convlast_ref[...] = ext_s[rows:rows + hist, :]
    ext_s[0:hist, :] = ext_s[rows:rows + hist, :]


def _layer_a(x, h0, conv0_tm, p):
    bn, t_len, _ = x.shape
    tt = LAYER_A_ROWS // bn
    rows = bn * tt
    hist = (CONV_WIDTH - 1) * bn
    grid = (t_len // tt,)
    kern = functools.partial(_layer_a_kernel, bn=bn, tt=tt)
    return pl.pallas_call(
        kern,
        grid=grid,
        in_specs=[
            pl.BlockSpec((bn, tt, D_MODEL), lambda i: (0, i, 0)),
            _const_spec((bn, LRU_WIDTH)),
            _const_spec((hist, LRU_WIDTH)),
            _const_spec((1, D_MODEL)),
            _const_spec((1, D_MODEL)),
            _const_spec((D_MODEL, LRU_WIDTH)),
            _const_spec((D_MODEL, LRU_WIDTH)),
            _const_spec((CONV_WIDTH, LRU_WIDTH)),
            _const_spec((1, LRU_WIDTH)),
            _const_spec((N_LRU_BLOCKS, LRU_BLOCK, 2 * LRU_BLOCK)),
            _const_spec((1, LRU_WIDTH)),
            _const_spec((1, LRU_WIDTH)),
            _const_spec((1, LRU_WIDTH)),
            _const_spec((LRU_WIDTH, D_MODEL)),
        ],
        out_specs=[
            pl.BlockSpec((bn, tt, D_MODEL), lambda i: (0, i, 0)),
            _const_spec((bn, LRU_WIDTH)),
            _const_spec((hist, LRU_WIDTH)),
        ],
        out_shape=[
            jax.ShapeDtypeStruct((bn, t_len, D_MODEL), f32),
            jax.ShapeDtypeStruct((bn, LRU_WIDTH), f32),
            jax.ShapeDtypeStruct((hist, LRU_WIDTH), f32),
        ],
        scratch_shapes=[
            pltpu.VMEM((rows, D_MODEL), f32),
            pltpu.VMEM((rows + hist, LRU_WIDTH), f32),
            pltpu.VMEM((rows, LRU_WIDTH), f32),
            pltpu.VMEM((rows, LRU_WIDTH), bf16),
            pltpu.VMEM((bn, LRU_WIDTH), f32),
        ],
        compiler_params=pltpu.CompilerParams(
            dimension_semantics=("arbitrary",), vmem_limit_bytes=VMEM_LIMIT),
        name="layer_a",
    )(x, h0, conv0_tm, p["g_pre0"], p["g_post0"], p["w_in_x"], p["w_in_g"], p["conv_w"],
      p["conv_b"], p["w_gates"], p["b_r"], p["b_i"], p["lam"], p["a_w_out"])


def _proj_kernel(h_ref, gkv_ref, gpre_ref, wkv_ref, wqg_ref,
                 k_ref, v_ref, kb_ref, vb_ref, q_ref, gate_ref):
    h = h_ref[...]
    hn = h * lax.rsqrt(jnp.mean(h * h, axis=-1, keepdims=True) + EPS)
    kv = jnp.dot((hn * gkv_ref[...]).astype(bf16), wkv_ref[...], preferred_element_type=f32)
    k = kv[:, :N_HEADS * KEY_DIM]
    v = kv[:, N_HEADS * KEY_DIM:]
    n_tok = h.shape[0]
    for hd in range(N_HEADS):
        k_ref[pl.ds(hd, n_tok, stride=N_HEADS), :] = k[:, hd * KEY_DIM:(hd + 1) * KEY_DIM]
        v_ref[pl.ds(hd, n_tok, stride=N_HEADS), :] = v[:, hd * VAL_DIM:(hd + 1) * VAL_DIM]
    kb_ref[...] = k.astype(bf16)
    vb_ref[...] = v.astype(bf16)
    qg = jnp.dot((hn * gpre_ref[...]).astype(bf16), wqg_ref[...], preferred_element_type=f32)
    q_ref[...] = (qg[:, :N_HEADS * KEY_DIM] * QK_SCALE).astype(bf16)
    gate_ref[...] = qg[:, N_HEADS * KEY_DIM:]


def _proj(h2d, p):
    n_rows = h2d.shape[0]
    tm = PROJ_ROWS
    row_spec = pl.BlockSpec((tm, D_MODEL), lambda i: (i, 0))
    head_row_spec = pl.BlockSpec((tm * N_HEADS, KEY_DIM), lambda i: (i, 0))
    return pl.pallas_call(
        _proj_kernel,
        grid=(n_rows // tm,),
        in_specs=[
            row_spec,
            _const_spec((1, D_MODEL)),
            _const_spec((1, D_MODEL)),
            _const_spec((D_MODEL, 2 * D_MODEL)),
            _const_spec((D_MODEL, 2 * D_MODEL)),
        ],
        out_specs=[head_row_spec] * 2 + [row_spec] * 4,
        out_shape=[
            jax.ShapeDtypeStruct((n_rows * N_HEADS, KEY_DIM), f32),
            jax.ShapeDtypeStruct((n_rows * N_HEADS, VAL_DIM), f32),
            jax.ShapeDtypeStruct((n_rows, D_MODEL), bf16),
            jax.ShapeDtypeStruct((n_rows, D_MODEL), bf16),
            jax.ShapeDtypeStruct((n_rows, D_MODEL), bf16),
            jax.ShapeDtypeStruct((n_rows, D_MODEL), f32),
        ],
        compiler_params=pltpu.CompilerParams(
            dimension_semantics=("arbitrary",), vmem_limit_bytes=VMEM_LIMIT),
        name="proj",
    )(h2d, p["g_kv"], p["g_pre1"], p["w_kv"], p["b_w_in"])


def _lambda_full(lamv_ref):
    lv = lamv_ref[...]
    e1 = jnp.exp(jnp.sum(lv[0:1] * lv[1:2], axis=-1, keepdims=True))
    e2 = jnp.exp(jnp.sum(lv[2:3] * lv[3:4], axis=-1, keepdims=True))
    return e1 - e2 + LAM_INIT


def _subln(o, subln):
    return (o * lax.rsqrt(jnp.mean(o * o, axis=-1, keepdims=True) + EPS) * subln) * (1.0 - LAM_INIT)


def _online_softmax_step(st, c, m_s, l_s, acc_s, value_fn):
    m_old = m_s[...]
    m_new = jnp.maximum(m_old, jnp.max(st, axis=0, keepdims=True) + c)
    p = jnp.exp(st - (m_new - c))
    corr = jnp.exp(m_old - m_new)
    l_s[...] = l_s[...] * corr + jnp.sum(p, axis=0, keepdims=True)
    acc_s[...] = acc_s[...] * corr + value_fn(p.astype(bf16))
    m_s[...] = m_new


def _attn_prompt_kernel(slopes_ref, q_ref, k_ref, v_ref, lamv_ref, subln_ref, o_ref,
                        vt_s, m_s, l_s, acc_s, *, tq):
    head = pl.program_id(1)
    qi = pl.program_id(2)
    slope = slopes_ref[head]
    cols = MAPS * tq

    @pl.when(qi == 0)
    def _():
        vt_s[...] = v_ref[...].T

    qt = q_ref[...].T
    feat = lax.broadcasted_iota(jnp.int32, (KEY_DIM, tq), 0)
    zero = jnp.zeros_like(qt)
    q2t = jnp.concatenate([jnp.where(feat < HEAD_DIM, qt, zero),
                           jnp.where(feat >= HEAD_DIM, qt, zero)], axis=1)

    krow = lax.broadcasted_iota(jnp.int32, (tq, cols), 0)
    qcol = lax.broadcasted_iota(jnp.int32, (tq, cols), 1)
    qcol = jnp.where(qcol >= tq, qcol - tq, qcol)
    key_bias = slope * krow.astype(f32)

    m_s[...] = jnp.full((1, cols), NEG_INF, f32)
    l_s[...] = jnp.zeros((1, cols), f32)
    acc_s[...] = jnp.zeros((VAL_DIM, cols), f32)

    def step(j, masked):
        start = pl.multiple_of(j * tq, tq)
        k = k_ref[pl.ds(start, tq), :]
        st = jnp.dot(k, q2t, preferred_element_type=f32) + key_bias
        if masked:
            st = jnp.where(qcol >= krow, st, NEG_INF)
        c = slope * ((j - qi) * tq).astype(f32)
        _online_softmax_step(
            st, c, m_s, l_s, acc_s,
            lambda p: jnp.dot(vt_s[:, pl.ds(start, tq)], p, preferred_element_type=f32))

    def body(j, carry):
        step(j, False)
        return carry

    lax.fori_loop(0, qi, body, 0)
    step(qi, True)

    lam = _lambda_full(lamv_ref)
    accn = acc_s[...] / l_s[...]
    ot = accn[:, :tq] - lam * accn[:, tq:]
    o_ref[...] = _subln(ot.T, subln_ref[...])


def _attn_prompt(q, kb, vb, p):
    b, t_len, _ = q.shape
    tq = ATTN_TQ
    kern = functools.partial(_attn_prompt_kernel, tq=tq)
    return pl.pallas_call(
        kern,
        grid=(b, N_HEADS, t_len // tq),
        in_specs=[
            pl.BlockSpec(memory_space=pltpu.SMEM),
            pl.BlockSpec((None, tq, KEY_DIM), lambda bi, h, qi: (bi, qi, h)),
            pl.BlockSpec((None, t_len, KEY_DIM), lambda bi, h, qi: (bi, 0, h)),
            pl.BlockSpec((None, t_len, VAL_DIM), lambda bi, h, qi: (bi, 0, h)),
            _const_spec((4, HEAD_DIM)),
            _const_spec((1, VAL_DIM)),
        ],
        out_specs=pl.BlockSpec((None, tq, VAL_DIM), lambda bi, h, qi: (bi, qi, h)),
        out_shape=jax.ShapeDtypeStruct((b, t_len, ATTN_WIDTH), f32),
        scratch_shapes=[
            pltpu.VMEM((VAL_DIM, t_len), bf16),
            pltpu.VMEM((1, MAPS * tq), f32),
            pltpu.VMEM((1, MAPS * tq), f32),
            pltpu.VMEM((VAL_DIM, MAPS * tq), f32),
        ],
        compiler_params=pltpu.CompilerParams(
            dimension_semantics=("arbitrary", "arbitrary", "arbitrary"),
            vmem_limit_bytes=VMEM_LIMIT),
        name="attn_prompt",
    )(p["slopes"], q, kb, vb, p["lamv"], p["subln"])


def _attn_decode_kernel(pt_ref, qt_ref, knew_ref, vnew_ref, slope_ref, pagebias_ref, newbias_ref,
                        lamv_ref, subln_ref, *rest, past_len):
    npg = PAGES_PER_STEP
    k_refs = rest[:npg]
    v_refs = rest[npg:2 * npg]
    o_ref = rest[2 * npg]
    m_s, l_s, acc_s = rest[2 * npg + 1:]
    i = pl.program_id(1)

    qt = qt_ref[...]
    slope = slope_ref[...]
    col = lax.broadcasted_iota(jnp.int32, (1, DEC_COLS), 1)
    tok = jnp.bitwise_and(col, DEC_TOKENS - 1)

    def page_step(k, v, bias, c):
        st = jnp.dot(k.astype(bf16), qt, preferred_element_type=f32) + bias
        vb = v.astype(bf16)
        _online_softmax_step(
            st, c, m_s, l_s, acc_s,
            lambda p: lax.dot_general(vb, p, (((0,), (0,)), ((), ())), preferred_element_type=f32))

    @pl.when(i == 0)
    def _():
        m_s[...] = jnp.full((1, DEC_COLS), NEG_INF, f32)
        l_s[...] = jnp.zeros((1, DEC_COLS), f32)
        acc_s[...] = jnp.zeros((VAL_DIM, DEC_COLS), f32)
        page_step(knew_ref[...], vnew_ref[...], newbias_ref[...], jnp.zeros((1, DEC_COLS), f32))

    for n in range(npg):
        page_start = (i * npg + n) * PAGE_SIZE
        c = -slope * (past_len + tok - page_start).astype(f32)
        page_step(k_refs[n][...].reshape(PAGE_ROWS, KEY_DIM),
                  v_refs[n][...].reshape(PAGE_ROWS, VAL_DIM), pagebias_ref[...], c)

    @pl.when(i == pl.num_programs(1) - 1)
    def _():
        lam = _lambda_full(lamv_ref)
        accn = (acc_s[...] / l_s[...]).T
        for h in range(N_HEADS):
            r0 = h * COLS_PER_HEAD
            o = accn[r0:r0 + DEC_TOKENS] - lam * accn[r0 + DEC_TOKENS:r0 + 2 * DEC_TOKENS]
            o_ref[:, h * VAL_DIM:(h + 1) * VAL_DIM] = _subln(o, subln_ref[...])


def _decode_bias_tables(slopes):
    col_head = jnp.arange(DEC_COLS) // COLS_PER_HEAD
    col_tok = jnp.arange(DEC_COLS) % DEC_TOKENS
    col_slope = slopes[col_head]

    def table(n_pos, fn):
        row = jnp.arange(n_pos * N_HEADS)
        pos = row // N_HEADS
        own = (row % N_HEADS)[:, None] == col_head[None, :]
        val, ok = fn(pos[:, None].astype(f32), pos[:, None])
        return jnp.where(own & ok, val, NEG_INF).astype(f32)

    page = table(PAGE_SIZE, lambda posf, pos: (col_slope[None, :] * posf, True))
    new = table(DEC_TOKENS, lambda posf, pos: (-col_slope[None, :] * (col_tok[None, :].astype(f32) - posf),
                                                pos <= col_tok[None, :]))
    return page, new, col_slope[None, :]


def _attn_decode(q, k_rows, v_rows, cache_k, cache_v, page_table, p):
    b, n_tok, _ = q.shape
    assert n_tok == DEC_TOKENS
    n_pages = page_table.shape[1]
    npg = PAGES_PER_STEP

    q5 = q.reshape(b, n_tok, N_HEADS, MAPS, HEAD_DIM).transpose(0, 3, 4, 2, 1)
    eye_m = jnp.eye(MAPS, dtype=q.dtype)
    qt = q5[:, :, :, :, None, :] * eye_m[None, :, None, None, :, None]
    qt = qt.reshape(b, KEY_DIM, DEC_COLS)

    def page_spec(n):
        return pl.BlockSpec((None, PAGE_SIZE, N_HEADS, KEY_DIM),
                            lambda bi, i, pt: (pt[bi, i * npg + n], 0, 0, 0))

    def const_spec(shape):
        return pl.BlockSpec(shape, lambda bi, i, pt: (0, 0))

    grid_spec = pltpu.PrefetchScalarGridSpec(
        num_scalar_prefetch=1,
        grid=(b, n_pages // npg),
        in_specs=[
            pl.BlockSpec((None, KEY_DIM, DEC_COLS), lambda bi, i, pt: (bi, 0, 0)),
            pl.BlockSpec((NEW_ROWS, KEY_DIM), lambda bi, i, pt: (bi, 0)),
            pl.BlockSpec((NEW_ROWS, VAL_DIM), lambda bi, i, pt: (bi, 0)),
            const_spec((1, DEC_COLS)),
            const_spec((PAGE_ROWS, DEC_COLS)),
            const_spec((NEW_ROWS, DEC_COLS)),
            const_spec((4, HEAD_DIM)),
            const_spec((1, VAL_DIM)),
        ] + [page_spec(n) for n in range(npg)] + [page_spec(n) for n in range(npg)],
        out_specs=pl.BlockSpec((None, n_tok, ATTN_WIDTH), lambda bi, i, pt: (bi, 0, 0)),
        scratch_shapes=[
            pltpu.VMEM((1, DEC_COLS), f32),
            pltpu.VMEM((1, DEC_COLS), f32),
            pltpu.VMEM((VAL_DIM, DEC_COLS), f32),
        ],
    )
    kern = functools.partial(_attn_decode_kernel, past_len=n_pages * PAGE_SIZE)
    return pl.pallas_call(
        kern,
        grid_spec=grid_spec,
        out_shape=jax.ShapeDtypeStruct((b, n_tok, ATTN_WIDTH), f32),
        compiler_params=pltpu.CompilerParams(
            dimension_semantics=("arbitrary", "arbitrary"), vmem_limit_bytes=VMEM_LIMIT),
        name="attn_decode",
    )(page_table, qt, k_rows, v_rows, p["slope_cols"], p["page_bias"], p["new_bias"], p["lamv"],
      p["subln"], *([cache_k] * npg), *([cache_v] * npg))


def _tail_kernel(o_ref, gate_ref, h_ref, wout_ref, gpost_ref, y_ref):
    z = (o_ref[...] * jax.nn.silu(gate_ref[...])).astype(bf16)
    out = jnp.dot(z, wout_ref[...], preferred_element_type=f32)
    y_ref[...] = h_ref[...] + _rms(out, gpost_ref[...])


def _tail(o2d, gate2d, h2d, p):
    n_rows = o2d.shape[0]
    tm = PROJ_ROWS
    row_spec = pl.BlockSpec((tm, D_MODEL), lambda i: (i, 0))
    return pl.pallas_call(
        _tail_kernel,
        grid=(n_rows // tm,),
        in_specs=[row_spec, row_spec, row_spec,
                  _const_spec((ATTN_WIDTH, D_MODEL)), _const_spec((1, D_MODEL))],
        out_specs=row_spec,
        out_shape=jax.ShapeDtypeStruct((n_rows, D_MODEL), f32),
        compiler_params=pltpu.CompilerParams(
            dimension_semantics=("arbitrary",), vmem_limit_bytes=VMEM_LIMIT),
        name="tail",
    )(o2d, gate2d, h2d, p["b_w_out"], p["g_post1"])


def _trunk(x, h0, conv0, p, attend):
    b, t_len, _ = x.shape
    conv0_tm = conv0.transpose(1, 0, 2).reshape((CONV_WIDTH - 1) * b, LRU_WIDTH)
    h1, h_last, conv_last = _layer_a(x, h0, conv0_tm, p)
    h2d = h1.reshape(b * t_len, D_MODEL)
    k, v, kb, vb, q, gate = _proj(h2d, p)
    o = attend(q.reshape(b, t_len, -1), k, v, kb.reshape(b, t_len, -1), vb.reshape(b, t_len, -1))
    y = _tail(o.reshape(b * t_len, ATTN_WIDTH), gate, h2d, p)
    new_conv = conv_last.reshape(CONV_WIDTH - 1, b, LRU_WIDTH).transpose(1, 0, 2)
    return (y.reshape(b, t_len, D_MODEL), k.reshape(b, t_len, N_HEADS, KEY_DIM),
            v.reshape(b, t_len, N_HEADS, VAL_DIM), h_last[None], new_conv[None])


def kernel(x_prompt, x_sample, cache_k, cache_v, page_table, state_lru_h, state_conv, norm_pre, norm_post, a_w_in, a_conv_w, a_conv_b, a_w_r, a_b_r, a_w_i, a_b_i, a_lambda, a_w_out, kv_norm, w_kv, b_w_in, b_lambda_q1, b_lambda_k1, b_lambda_q2, b_lambda_k2, b_subln, b_w_out):
    slopes = jnp.asarray([2.0 ** (-8.0 * (h + 1) / N_HEADS) for h in range(N_HEADS)], dtype=f32)
    page_bias, new_bias, slope_cols = _decode_bias_tables(slopes)
    p = dict(
        g_pre0=norm_pre[0:1], g_post0=norm_post[0:1], g_pre1=norm_pre[1:2], g_post1=norm_post[1:2],
        w_in_x=a_w_in[0, :, :LRU_WIDTH].astype(bf16), w_in_g=a_w_in[0, :, LRU_WIDTH:].astype(bf16),
        conv_w=a_conv_w[0], conv_b=a_conv_b[0:1],
        w_gates=jnp.concatenate([a_w_r[0], a_w_i[0]], axis=-1).astype(bf16),
        b_r=a_b_r[0:1], b_i=a_b_i[0:1], lam=a_lambda[0:1], a_w_out=a_w_out[0].astype(bf16),
        g_kv=kv_norm[None, :], w_kv=w_kv.astype(bf16), b_w_in=b_w_in[0].astype(bf16),
        lamv=jnp.concatenate([b_lambda_q1, b_lambda_k1, b_lambda_q2, b_lambda_k2], axis=0),
        subln=b_subln[0:1], b_w_out=b_w_out[0].astype(bf16),
        slopes=slopes, slope_cols=slope_cols, page_bias=page_bias, new_bias=new_bias,
    )

    def attend_prompt(q, k_rows, v_rows, kb, vb):
        return _attn_prompt(q, kb, vb, p)

    def attend_sample(q, k_rows, v_rows, kb, vb):
        return _attn_decode(q, k_rows, v_rows, cache_k, cache_v, page_table, p)

    bp = x_prompt.shape[0]
    h0_prompt = jnp.zeros((bp, LRU_WIDTH), f32)
    conv0_prompt = jnp.zeros((bp, CONV_WIDTH - 1, LRU_WIDTH), f32)
    y_p, k_p, v_p, lru_p, conv_p = _trunk(x_prompt, h0_prompt, conv0_prompt, p, attend_prompt)
    y_s, k_s, v_s, lru_s, conv_s = _trunk(x_sample, state_lru_h[0], state_conv[0], p, attend_sample)
    return (y_p, y_s, k_p, v_p, lru_p, conv_p, k_s, v_s, lru_s, conv_s)
```

```python
import functools
import math

import jax
import jax.numpy as jnp
from jax import lax
from jax.experimental import pallas as pl
from jax.experimental.pallas import tpu as pltpu

f32 = jnp.float32
bf16 = jnp.bfloat16

D_MODEL = 1024
LRU_WIDTH = D_MODEL
N_LRU_BLOCKS = 8
LRU_BLOCK = LRU_WIDTH // N_LRU_BLOCKS
CONV_WIDTH = 4
LRU_C = 8.0
N_HEADS = 8
HEAD_DIM = D_MODEL // (2 * N_HEADS)
KEY_DIM = 2 * HEAD_DIM
VAL_DIM = 2 * HEAD_DIM
ATTN_WIDTH = N_HEADS * VAL_DIM
PAGE_SIZE = 128
EPS = 1e-6
NEG_INF = -1e30
LAM_INIT = 0.8 - 0.6 * math.exp(-0.3 * 1)
LOG2E = math.log2(math.e)
QK_SCALE = HEAD_DIM ** -0.5 * LOG2E

VMEM_LIMIT = 48 * 1024 * 1024

LAYER_A_ROWS = 256
PROJ_ROWS = 256
ATTN_TQ = 512
PAGES_PER_STEP = 16
PAGES_PER_TILE = 16
MAPS = 2
DEC_TOKENS = 8
COLS_PER_HEAD = MAPS * DEC_TOKENS
DEC_COLS = N_HEADS * COLS_PER_HEAD
PAGE_ROWS = PAGE_SIZE * N_HEADS
NEW_ROWS = DEC_TOKENS * N_HEADS


def _const_spec(shape):
    n = len(shape)
    return pl.BlockSpec(shape, lambda *_: (0,) * n)


def _rms(x, g):
    ms = jnp.mean(x * x, axis=-1, keepdims=True)
    return x * lax.rsqrt(ms + EPS) * g


def _layer_a_kernel(x_ref, h0_ref, conv0_ref, gpre_ref, gpost_ref, winx_ref, wing_ref,
                    cw_ref, cb_ref, wg_ref, br_ref, bi_ref, lam_ref, wout_ref,
                    h1_ref, hlast_ref, convlast_ref,
                    x_s, ext_s, gate_s, y_s, hc_s, *, bn, tt):
    i = pl.program_id(0)
    rows = bn * tt
    hist = (CONV_WIDTH - 1) * bn

    @pl.when(i == 0)
    def _():
        ext_s[0:hist, :] = conv0_ref[...]
        hc_s[...] = h0_ref[...]

    x = jnp.swapaxes(x_ref[...], 0, 1).reshape(rows, D_MODEL)
    x_s[...] = x
    xn = _rms(x, gpre_ref[...]).astype(bf16)
    ext_s[hist:hist + rows, :] = jnp.dot(xn, winx_ref[...], preferred_element_type=f32)
    gate_s[...] = jnp.dot(xn, wing_ref[...], preferred_element_type=f32)

    for j in range(N_LRU_BLOCKS):
        sl = slice(j * LRU_BLOCK, (j + 1) * LRU_BLOCK)
        xc = cb_ref[:, sl]
        for k in range(CONV_WIDTH):
            xc = xc + ext_s[k * bn:k * bn + rows, sl] * cw_ref[k:k + 1, sl]
        g = jnp.dot(xc.astype(bf16), wg_ref[j], preferred_element_type=f32)
        r = jax.nn.sigmoid(g[:, :LRU_BLOCK] + br_ref[:, sl])
        gi = jax.nn.sigmoid(g[:, LRU_BLOCK:] + bi_ref[:, sl])
        log_a = -LRU_C * r * jax.nn.softplus(-lam_ref[:, sl])
        a = jnp.exp(log_a)
        mult = jnp.sqrt(jnp.tanh(-log_a) * (1.0 + a * a))
        bterm = mult * (gi * xc)
        h = hc_s[:, sl]
        hs = []
        for t in range(tt):
            h = a[t * bn:(t + 1) * bn] * h + bterm[t * bn:(t + 1) * bn]
            hs.append(h)
        hc_s[:, sl] = h
        hseq = jnp.concatenate(hs, axis=0)
        y_s[:, sl] = (hseq * jax.nn.silu(gate_s[:, sl])).astype(bf16)

    out = jnp.dot(y_s[...], wout_ref[...], preferred_element_type=f32)
    h1 = x_s[...] + _rms(out, gpost_ref[...])
    h1_ref[...] = jnp.swapaxes(h1.reshape(tt, bn, D_MODEL), 0, 1)
    hlast_ref[...] = hc_s[...]
    convlast_ref[...] = ext_s[rows:rows + hist, :]
    ext_s[0:hist, :] = ext_s[rows:rows + hist, :]


def _layer_a(x, h0, conv0_tm, p):
    bn, t_len, _ = x.shape
    tt = LAYER_A_ROWS // bn
    rows = bn * tt
    hist = (CONV_WIDTH - 1) * bn
    grid = (t_len // tt,)
    kern = functools.partial(_layer_a_kernel, bn=bn, tt=tt)
    return pl.pallas_call(
        kern,
        grid=grid,
        in_specs=[
            pl.BlockSpec((bn, tt, D_MODEL), lambda i: (0, i, 0)),
            _const_spec((bn, LRU_WIDTH)),
            _const_spec((hist, LRU_WIDTH)),
            _const_spec((1, D_MODEL)),
            _const_spec((1, D_MODEL)),
            _const_spec((D_MODEL, LRU_WIDTH)),
            _const_spec((D_MODEL, LRU_WIDTH)),
            _const_spec((CONV_WIDTH, LRU_WIDTH)),
            _const_spec((1, LRU_WIDTH)),
            _const_spec((N_LRU_BLOCKS, LRU_BLOCK, 2 * LRU_BLOCK)),
            _const_spec((1, LRU_WIDTH)),
            _const_spec((1, LRU_WIDTH)),
            _const_spec((1, LRU_WIDTH)),
            _const_spec((LRU_WIDTH, D_MODEL)),
        ],
        out_specs=[
            pl.BlockSpec((bn, tt, D_MODEL), lambda i: (0, i, 0)),
            _const_spec((bn, LRU_WIDTH)),
            _const_spec((hist, LRU_WIDTH)),
        ],
        out_shape=[
            jax.ShapeDtypeStruct((bn, t_len, D_MODEL), f32),
            jax.ShapeDtypeStruct((bn, LRU_WIDTH), f32),
            jax.ShapeDtypeStruct((hist, LRU_WIDTH), f32),
        ],
        scratch_shapes=[
            pltpu.VMEM((rows, D_MODEL), f32),
            pltpu.VMEM((rows + hist, LRU_WIDTH), f32),
            pltpu.VMEM((rows, LRU_WIDTH), f32),
            pltpu.VMEM((rows, LRU_WIDTH), bf16),
            pltpu.VMEM((bn, LRU_WIDTH), f32),
        ],
        compiler_params=pltpu.CompilerParams(
            dimension_semantics=("arbitrary",), vmem_limit_bytes=VMEM_LIMIT),
        name="layer_a",
    )(x, h0, conv0_tm, p["g_pre0"], p["g_post0"], p["w_in_x"], p["w_in_g"], p["conv_w"],
      p["conv_b"], p["w_gates"], p["b_r"], p["b_i"], p["lam"], p["a_w_out"])


def _proj_kernel(h_ref, gkv_ref, gpre_ref, wkv_ref, wqg_ref,
                 k_ref, v_ref, kb_ref, vb_ref, q_ref, gate_ref):
    h = h_ref[...]
    hn = h * lax.rsqrt(jnp.mean(h * h, axis=-1, keepdims=True) + EPS)
    kv = jnp.dot((hn * gkv_ref[...]).astype(bf16), wkv_ref[...], preferred_element_type=f32)
    k = kv[:, :N_HEADS * KEY_DIM]
    v = kv[:, N_HEADS * KEY_DIM:]
    n_tok = h.shape[0]
    for hd in range(N_HEADS):
        k_ref[pl.ds(hd, n_tok, stride=N_HEADS), :] = k[:, hd * KEY_DIM:(hd + 1) * KEY_DIM]
        v_ref[pl.ds(hd, n_tok, stride=N_HEADS), :] = v[:, hd * VAL_DIM:(hd + 1) * VAL_DIM]
    kb_ref[...] = k.astype(bf16)
    vb_ref[...] = v.astype(bf16)
    qg = jnp.dot((hn * gpre_ref[...]).astype(bf16), wqg_ref[...], preferred_element_type=f32)
    q_ref[...] = (qg[:, :N_HEADS * KEY_DIM] * QK_SCALE).astype(bf16)
    gate_ref[...] = qg[:, N_HEADS * KEY_DIM:]


def _proj(h2d, p):
    n_rows = h2d.shape[0]
    tm = PROJ_ROWS
    row_spec = pl.BlockSpec((tm, D_MODEL), lambda i: (i, 0))
    head_row_spec = pl.BlockSpec((tm * N_HEADS, KEY_DIM), lambda i: (i, 0))
    return pl.pallas_call(
        _proj_kernel,
        grid=(n_rows // tm,),
        in_specs=[
            row_spec,
            _const_spec((1, D_MODEL)),
            _const_spec((1, D_MODEL)),
            _const_spec((D_MODEL, 2 * D_MODEL)),
            _const_spec((D_MODEL, 2 * D_MODEL)),
        ],
        out_specs=[head_row_spec] * 2 + [row_spec] * 4,
        out_shape=[
            jax.ShapeDtypeStruct((n_rows * N_HEADS, KEY_DIM), f32),
            jax.ShapeDtypeStruct((n_rows * N_HEADS, VAL_DIM), f32),
            jax.ShapeDtypeStruct((n_rows, D_MODEL), bf16),
            jax.ShapeDtypeStruct((n_rows, D_MODEL), bf16),
            jax.ShapeDtypeStruct((n_rows, D_MODEL), bf16),
            jax.ShapeDtypeStruct((n_rows, D_MODEL), f32),
        ],
        compiler_params=pltpu.CompilerParams(
            dimension_semantics=("arbitrary",), vmem_limit_bytes=VMEM_LIMIT),
        name="proj",
    )(h2d, p["g_kv"], p["g_pre1"], p["w_kv"], p["b_w_in"])


def _lambda_full(lamv_ref):
    lv = lamv_ref[...]
    e1 = jnp.exp(jnp.sum(lv[0:1] * lv[1:2], axis=-1, keepdims=True))
    e2 = jnp.exp(jnp.sum(lv[2:3] * lv[3:4], axis=-1, keepdims=True))
    return e1 - e2 + LAM_INIT


def _subln(o, subln):
    return (o * lax.rsqrt(jnp.mean(o * o, axis=-1, keepdims=True) + EPS) * subln) * (1.0 - LAM_INIT)


def _online_softmax_step(st, c, m_s, l_s, acc_s, value_fn):
    m_old = m_s[...]
    m_new = jnp.maximum(m_old, jnp.max(st, axis=0, keepdims=True) + c)
    p = jnp.exp2(st - (m_new - c))
    corr = jnp.exp2(m_old - m_new)
    l_s[...] = l_s[...] * corr + jnp.sum(p, axis=0, keepdims=True)
    acc_s[...] = acc_s[...] * corr + value_fn(p)
    m_s[...] = m_new


def _attn_prompt_kernel(slopes_ref, q_ref, k_ref, v_ref, lamv_ref, subln_ref, o_ref,
                        vt_s, m_s, l_s, acc_s, *, tq):
    head = pl.program_id(1)
    qi = pl.program_id(2)
    slope = slopes_ref[head]
    cols = MAPS * tq

    @pl.when(qi == 0)
    def _():
        vt_s[...] = v_ref[...].T

    qt = q_ref[...].T
    feat = lax.broadcasted_iota(jnp.int32, (KEY_DIM, tq), 0)
    zero = jnp.zeros_like(qt)
    q2t = jnp.concatenate([jnp.where(feat < HEAD_DIM, qt, zero),
                           jnp.where(feat >= HEAD_DIM, qt, zero)], axis=1)

    krow = lax.broadcasted_iota(jnp.int32, (tq, cols), 0)
    qcol = lax.broadcasted_iota(jnp.int32, (tq, cols), 1)
    qcol = jnp.where(qcol >= tq, qcol - tq, qcol)
    key_bias = slope * krow.astype(f32)

    m_s[...] = jnp.full((1, cols), NEG_INF, f32)
    l_s[...] = jnp.zeros((1, cols), f32)
    acc_s[...] = jnp.zeros((VAL_DIM, cols), f32)

    def step(j, masked):
        start = pl.multiple_of(j * tq, tq)
        k = k_ref[pl.ds(start, tq), :]
        st = jnp.dot(k, q2t, preferred_element_type=f32) + key_bias
        if masked:
            st = jnp.where(qcol >= krow, st, NEG_INF)
        c = slope * ((j - qi) * tq).astype(f32)
        _online_softmax_step(
            st, c, m_s, l_s, acc_s,
            lambda p: jnp.dot(vt_s[:, pl.ds(start, tq)], p.astype(bf16),
                              preferred_element_type=f32))

    def body(j, carry):
        step(j, False)
        return carry

    lax.fori_loop(0, qi, body, 0)
    step(qi, True)

    lam = _lambda_full(lamv_ref)
    accn = acc_s[...] / l_s[...]
    ot = accn[:, :tq] - lam * accn[:, tq:]
    o_ref[...] = _subln(ot.T, subln_ref[...])


def _attn_prompt(q, kb, vb, p):
    b, t_len, _ = q.shape
    tq = ATTN_TQ
    kern = functools.partial(_attn_prompt_kernel, tq=tq)
    return pl.pallas_call(
        kern,
        grid=(b, N_HEADS, t_len // tq),
        in_specs=[
            pl.BlockSpec(memory_space=pltpu.SMEM),
            pl.BlockSpec((None, tq, KEY_DIM), lambda bi, h, qi: (bi, qi, h)),
            pl.BlockSpec((None, t_len, KEY_DIM), lambda bi, h, qi: (bi, 0, h)),
            pl.BlockSpec((None, t_len, VAL_DIM), lambda bi, h, qi: (bi, 0, h)),
            _const_spec((4, HEAD_DIM)),
            _const_spec((1, VAL_DIM)),
        ],
        out_specs=pl.BlockSpec((None, tq, VAL_DIM), lambda bi, h, qi: (bi, qi, h)),
        out_shape=jax.ShapeDtypeStruct((b, t_len, ATTN_WIDTH), f32),
        scratch_shapes=[
            pltpu.VMEM((VAL_DIM, t_len), bf16),
            pltpu.VMEM((1, MAPS * tq), f32),
            pltpu.VMEM((1, MAPS * tq), f32),
            pltpu.VMEM((VAL_DIM, MAPS * tq), f32),
        ],
        compiler_params=pltpu.CompilerParams(
            dimension_semantics=("arbitrary", "arbitrary", "arbitrary"),
            vmem_limit_bytes=VMEM_LIMIT),
        name="attn_prompt",
    )(p["slopes"], q, kb, vb, p["lamv"], p["subln"])


def _attn_decode_kernel(pt_ref, qt_ref, knew_ref, vnew_ref, slope_ref, posbias_ref, newbias_ref,
                        lamv_ref, subln_ref, *rest, past_len):
    npg = PAGES_PER_STEP
    k_refs = rest[:npg]
    v_refs = rest[npg:2 * npg]
    o_ref = rest[2 * npg]
    qbd_s, m_s, l_s, acc_s = rest[2 * npg + 1:]
    i = pl.program_id(1)

    slope = slope_ref[...]
    col = lax.broadcasted_iota(jnp.int32, (1, DEC_COLS), 1)
    tok = jnp.bitwise_and(col, DEC_TOKENS - 1)

    @pl.when(i == 0)
    def _():
        qt = qt_ref[...]
        col_head = lax.broadcasted_iota(jnp.int32, (KEY_DIM, DEC_COLS), 1) // COLS_PER_HEAD
        for h in range(N_HEADS):
            qbd_s[h * KEY_DIM:(h + 1) * KEY_DIM, :] = jnp.where(col_head == h, qt, jnp.zeros_like(qt))
        m_s[...] = jnp.full((1, DEC_COLS), NEG_INF, f32)
        l_s[...] = jnp.zeros((1, DEC_COLS), f32)
        acc_s[...] = jnp.zeros((VAL_DIM, DEC_COLS), f32)
        st = jnp.dot(knew_ref[...].astype(bf16), qt, preferred_element_type=f32) + newbias_ref[...]
        vb = vnew_ref[...].astype(bf16)
        _online_softmax_step(
            st, jnp.zeros((1, DEC_COLS), f32), m_s, l_s, acc_s,
            lambda p: lax.dot_general(vb, p.astype(bf16), (((0,), (0,)), ((), ())),
                                      preferred_element_type=f32))

    def head_rows(ref, h):
        return ref[pl.ds(h, PAGE_SIZE, stride=N_HEADS), :].astype(bf16)

    c = -slope * (past_len + tok - i * (npg * PAGE_SIZE)).astype(f32)

    for g in range(0, npg, PAGES_PER_TILE):
        kg, vg = k_refs[g:g + PAGES_PER_TILE], v_refs[g:g + PAGES_PER_TILE]
        kcat = jnp.concatenate(
            [jnp.concatenate([head_rows(k_ref, h) for h in range(N_HEADS)], axis=1) for k_ref in kg],
            axis=0)
        st = (jnp.dot(kcat, qbd_s[...], preferred_element_type=f32)
              + posbias_ref[g * PAGE_SIZE:(g + PAGES_PER_TILE) * PAGE_SIZE, :])

        def values(p, vg=vg):
            pt = p.T.astype(bf16)
            blocks = []
            for h in range(0, N_HEADS, 2):
                vpair = jnp.concatenate(
                    [jnp.concatenate([head_rows(v_ref, hh) for v_ref in vg], axis=0)
                     for hh in (h, h + 1)], axis=1)
                prod = jnp.dot(pt[h * COLS_PER_HEAD:(h + 2) * COLS_PER_HEAD, :], vpair,
                               preferred_element_type=f32)
                blocks.append(prod[:COLS_PER_HEAD, :VAL_DIM])
                blocks.append(prod[COLS_PER_HEAD:, VAL_DIM:])
            return jnp.concatenate(blocks, axis=0).T

        _online_softmax_step(st, c, m_s, l_s, acc_s, values)

    @pl.when(i == pl.num_programs(1) - 1)
    def _():
        lam = _lambda_full(lamv_ref)
        accn = (acc_s[...] / l_s[...]).T
        for h in range(N_HEADS):
            r0 = h * COLS_PER_HEAD
            o = accn[r0:r0 + DEC_TOKENS] - lam * accn[r0 + DEC_TOKENS:r0 + 2 * DEC_TOKENS]
            o_ref[:, h * VAL_DIM:(h + 1) * VAL_DIM] = _subln(o, subln_ref[...])


def _decode_bias_tables(slopes):
    col_head = jnp.arange(DEC_COLS) // COLS_PER_HEAD
    col_tok = jnp.arange(DEC_COLS) % DEC_TOKENS
    col_slope = slopes[col_head]
    page = col_slope[None, :] * jnp.arange(PAGES_PER_STEP * PAGE_SIZE, dtype=f32)[:, None]
    row = jnp.arange(NEW_ROWS)
    row_tok = (row // N_HEADS)[:, None]
    ok = ((row % N_HEADS)[:, None] == col_head[None, :]) & (row_tok <= col_tok[None, :])
    new = jnp.where(ok, -col_slope[None, :] * (col_tok[None, :] - row_tok).astype(f32), NEG_INF)
    return page.astype(f32), new.astype(f32), col_slope[None, :]


def _attn_decode(q, k_rows, v_rows, cache_k, cache_v, page_table, p):
    b, n_tok, _ = q.shape
    assert n_tok == DEC_TOKENS
    n_pages = page_table.shape[1]
    npg = PAGES_PER_STEP

    q5 = q.reshape(b, n_tok, N_HEADS, MAPS, HEAD_DIM).transpose(0, 3, 4, 2, 1)
    eye_m = jnp.eye(MAPS, dtype=q.dtype)
    qt = q5[:, :, :, :, None, :] * eye_m[None, :, None, None, :, None]
    qt = qt.reshape(b, KEY_DIM, DEC_COLS)

    n_pool = cache_k.shape[0]
    ck = cache_k.reshape(n_pool * PAGE_ROWS, KEY_DIM)
    cv = cache_v.reshape(n_pool * PAGE_ROWS, VAL_DIM)

    def page_spec(n):
        return pl.BlockSpec((PAGE_ROWS, KEY_DIM), lambda bi, i, pt: (pt[bi, i * npg + n], 0))

    def const_spec(shape):
        return pl.BlockSpec(shape, lambda bi, i, pt: (0, 0))

    grid_spec = pltpu.PrefetchScalarGridSpec(
        num_scalar_prefetch=1,
        grid=(b, n_pages // npg),
        in_specs=[
            pl.BlockSpec((None, KEY_DIM, DEC_COLS), lambda bi, i, pt: (bi, 0, 0)),
            pl.BlockSpec((NEW_ROWS, KEY_DIM), lambda bi, i, pt: (bi, 0)),
            pl.BlockSpec((NEW_ROWS, VAL_DIM), lambda bi, i, pt: (bi, 0)),
            const_spec((1, DEC_COLS)),
            const_spec((npg * PAGE_SIZE, DEC_COLS)),
            const_spec((NEW_ROWS, DEC_COLS)),
            const_spec((4, HEAD_DIM)),
            const_spec((1, VAL_DIM)),
        ] + [page_spec(n) for n in range(npg)] + [page_spec(n) for n in range(npg)],
        out_specs=pl.BlockSpec((None, n_tok, ATTN_WIDTH), lambda bi, i, pt: (bi, 0, 0)),
        scratch_shapes=[
            pltpu.VMEM((N_HEADS * KEY_DIM, DEC_COLS), bf16),
            pltpu.VMEM((1, DEC_COLS), f32),
            pltpu.VMEM((1, DEC_COLS), f32),
            pltpu.VMEM((VAL_DIM, DEC_COLS), f32),
        ],
    )
    kern = functools.partial(_attn_decode_kernel, past_len=n_pages * PAGE_SIZE)
    return pl.pallas_call(
        kern,
        grid_spec=grid_spec,
        out_shape=jax.ShapeDtypeStruct((b, n_tok, ATTN_WIDTH), f32),
        compiler_params=pltpu.CompilerParams(
            dimension_semantics=("arbitrary", "arbitrary"), vmem_limit_bytes=VMEM_LIMIT),
        name="attn_decode",
    )(page_table, qt, k_rows, v_rows, p["slope_cols"], p["page_bias"], p["new_bias"], p["lamv"],
      p["subln"], *([ck] * npg), *([cv] * npg))


def _tail_kernel(o_ref, gate_ref, h_ref, wout_ref, gpost_ref, y_ref):
    z = (o_ref[...] * jax.nn.silu(gate_ref[...])).astype(bf16)
    out = jnp.dot(z, wout_ref[...], preferred_element_type=f32)
    y_ref[...] = h_ref[...] + _rms(out, gpost_ref[...])


def _tail(o2d, gate2d, h2d, p):
    n_rows = o2d.shape[0]
    tm = PROJ_ROWS
    row_spec = pl.BlockSpec((tm, D_MODEL), lambda i: (i, 0))
    return pl.pallas_call(
        _tail_kernel,
        grid=(n_rows // tm,),
        in_specs=[row_spec, row_spec, row_spec,
                  _const_spec((ATTN_WIDTH, D_MODEL)), _const_spec((1, D_MODEL))],
        out_specs=row_spec,
        out_shape=jax.ShapeDtypeStruct((n_rows, D_MODEL), f32),
        compiler_params=pltpu.CompilerParams(
            dimension_semantics=("arbitrary",), vmem_limit_bytes=VMEM_LIMIT),
        name="tail",
    )(o2d, gate2d, h2d, p["b_w_out"], p["g_post1"])


def _trunk(x, h0, conv0, p, attend):
    b, t_len, _ = x.shape
    conv0_tm = conv0.transpose(1, 0, 2).reshape((CONV_WIDTH - 1) * b, LRU_WIDTH)
    h1, h_last, conv_last = _layer_a(x, h0, conv0_tm, p)
    h2d = h1.reshape(b * t_len, D_MODEL)
    k, v, kb, vb, q, gate = _proj(h2d, p)
    o = attend(q.reshape(b, t_len, -1), k, v, kb.reshape(b, t_len, -1), vb.reshape(b, t_len, -1))
    y = _tail(o.reshape(b * t_len, ATTN_WIDTH), gate, h2d, p)
    new_conv = conv_last.reshape(CONV_WIDTH - 1, b, LRU_WIDTH).transpose(1, 0, 2)
    return (y.reshape(b, t_len, D_MODEL), k.reshape(b, t_len, N_HEADS, KEY_DIM),
            v.reshape(b, t_len, N_HEADS, VAL_DIM), h_last[None], new_conv[None])


def kernel(x_prompt, x_sample, cache_k, cache_v, page_table, state_lru_h, state_conv, norm_pre, norm_post, a_w_in, a_conv_w, a_conv_b, a_w_r, a_b_r, a_w_i, a_b_i, a_lambda, a_w_out, kv_norm, w_kv, b_w_in, b_lambda_q1, b_lambda_k1, b_lambda_q2, b_lambda_k2, b_subln, b_w_out):
    slopes = jnp.asarray([2.0 ** (-8.0 * (h + 1) / N_HEADS) for h in range(N_HEADS)], dtype=f32) * LOG2E
    page_bias, new_bias, slope_cols = _decode_bias_tables(slopes)
    p = dict(
        g_pre0=norm_pre[0:1], g_post0=norm_post[0:1], g_pre1=norm_pre[1:2], g_post1=norm_post[1:2],
        w_in_x=a_w_in[0, :, :LRU_WIDTH].astype(bf16), w_in_g=a_w_in[0, :, LRU_WIDTH:].astype(bf16),
        conv_w=a_conv_w[0], conv_b=a_conv_b[0:1],
        w_gates=jnp.concatenate([a_w_r[0], a_w_i[0]], axis=-1).astype(bf16),
        b_r=a_b_r[0:1], b_i=a_b_i[0:1], lam=a_lambda[0:1], a_w_out=a_w_out[0].astype(bf16),
        g_kv=kv_norm[None, :], w_kv=w_kv.astype(bf16), b_w_in=b_w_in[0].astype(bf16),
        lamv=jnp.concatenate([b_lambda_q1, b_lambda_k1, b_lambda_q2, b_lambda_k2], axis=0),
        subln=b_subln[0:1], b_w_out=b_w_out[0].astype(bf16),
        slopes=slopes, slope_cols=slope_cols, page_bias=page_bias, new_bias=new_bias,
    )

    def attend_prompt(q, k_rows, v_rows, kb, vb):
        return _attn_prompt(q, kb, vb, p)

    def attend_sample(q, k_rows, v_rows, kb, vb):
        return _attn_decode(q, k_rows, v_rows, cache_k, cache_v, page_table, p)

    bp = x_prompt.shape[0]
    h0_prompt = jnp.zeros((bp, LRU_WIDTH), f32)
    conv0_prompt = jnp.zeros((bp, CONV_WIDTH - 1, LRU_WIDTH), f32)
    y_p, k_p, v_p, lru_p, conv_p = _trunk(x_prompt, h0_prompt, conv0_prompt, p, attend_prompt)
    y_s, k_s, v_s, lru_s, conv_s = _trunk(x_sample, state_lru_h[0], state_conv[0], p, attend_sample)
    return (y_p, y_s, k_p, v_p, lru_p, conv_p, k_s, v_s, lru_s, conv_s)
```

```python
import functools
import math

import jax
import jax.numpy as jnp
from jax import lax
from jax.experimental import pallas as pl
from jax.experimental.pallas import tpu as pltpu

f32 = jnp.float32
bf16 = jnp.bfloat16

D_MODEL = 1024
LRU_WIDTH = D_MODEL
N_LRU_BLOCKS = 8
LRU_BLOCK = LRU_WIDTH // N_LRU_BLOCKS
CONV_WIDTH = 4
LRU_C = 8.0
N_HEADS = 8
HEAD_DIM = D_MODEL // (2 * N_HEADS)
KEY_DIM = 2 * HEAD_DIM
VAL_DIM = 2 * HEAD_DIM
ATTN_WIDTH = N_HEADS * VAL_DIM
PAGE_SIZE = 128
EPS = 1e-6
NEG_INF = -1e30
LAM_INIT = 0.8 - 0.6 * math.exp(-0.3 * 1)
LOG2E = math.log2(math.e)
QK_SCALE = HEAD_DIM ** -0.5 * LOG2E

VMEM_LIMIT = 48 * 1024 * 1024

LAYER_A_ROWS = 256
PROJ_ROWS = 256
ATTN_TQ = 512
PAGES_PER_STEP = 16
PAGES_PER_TILE = 16
MAPS = 2
DEC_TOKENS = 8
COLS_PER_HEAD = MAPS * DEC_TOKENS
DEC_COLS = N_HEADS * COLS_PER_HEAD
PAGE_ROWS = PAGE_SIZE * N_HEADS
NEW_ROWS = DEC_TOKENS * N_HEADS


def _const_spec(shape):
    n = len(shape)
    return pl.BlockSpec(shape, lambda *_: (0,) * n)


def _rms(x, g):
    ms = jnp.mean(x * x, axis=-1, keepdims=True)
    return x * lax.rsqrt(ms + EPS) * g


def _layer_a_kernel(x_ref, h0_ref, conv0_ref, gpre_ref, gpost_ref, winx_ref, wing_ref,
                    cw_ref, cb_ref, wg_ref, br_ref, bi_ref, lam_ref, wout_ref,
                    h1_ref, hlast_ref, convlast_ref,
                    x_s, ext_s, gate_s, y_s, hc_s, *, bn, tt):
    i = pl.program_id(0)
    rows = bn * tt
    hist = (CONV_WIDTH - 1) * bn

    @pl.when(i == 0)
    def _():
        ext_s[0:hist, :] = conv0_ref[...]
        hc_s[...] = h0_ref[...]

    x = jnp.swapaxes(x_ref[...], 0, 1).reshape(rows, D_MODEL)
    x_s[...] = x
    xn = _rms(x, gpre_ref[...]).astype(bf16)
    ext_s[hist:hist + rows, :] = jnp.dot(xn, winx_ref[...], preferred_element_type=f32)
    gate_s[...] = jnp.dot(xn, wing_ref[...], preferred_element_type=f32)

    for j in range(N_LRU_BLOCKS):
        sl = slice(j * LRU_BLOCK, (j + 1) * LRU_BLOCK)
        xc = cb_ref[:, sl]
        for k in range(CONV_WIDTH):
            xc = xc + ext_s[k * bn:k * bn + rows, sl] * cw_ref[k:k + 1, sl]
        g = jnp.dot(xc.astype(bf16), wg_ref[j], preferred_element_type=f32)
        r = jax.nn.sigmoid(g[:, :LRU_BLOCK] + br_ref[:, sl])
        gi = jax.nn.sigmoid(g[:, LRU_BLOCK:] + bi_ref[:, sl])
        log_a = -LRU_C * r * jax.nn.softplus(-lam_ref[:, sl])
        a = jnp.exp(log_a)
        mult = jnp.sqrt(jnp.tanh(-log_a) * (1.0 + a * a))
        bterm = mult * (gi * xc)
        h = hc_s[:, sl]
        hs = []
        for t in range(tt):
            h = a[t * bn:(t + 1) * bn] * h + bterm[t * bn:(t + 1) * bn]
            hs.append(h)
        hc_s[:, sl] = h
        hseq = jnp.concatenate(hs, axis=0)
        y_s[:, sl] = (hseq * jax.nn.silu(gate_s[:, sl])).astype(bf16)

    out = jnp.dot(y_s[...], wout_ref[...], preferred_element_type=f32)
    h1 = x_s[...] + _rms(out, gpost_ref[...])
    h1_ref[...] = jnp.swapaxes(h1.reshape(tt, bn, D_MODEL), 0, 1)
    hlast_ref[...] = hc_s[...]
    convlast_ref[...] = ext_s[rows:rows + hist, :]
    ext_s[0:hist, :] = ext_s[rows:rows + hist, :]


def _layer_a(x, h0, conv0_tm, p):
    bn, t_len, _ = x.shape
    tt = LAYER_A_ROWS // bn
    rows = bn * tt
    hist = (CONV_WIDTH - 1) * bn
    grid = (t_len // tt,)
    kern = functools.partial(_layer_a_kernel, bn=bn, tt=tt)
    return pl.pallas_call(
        kern,
        grid=grid,
        in_specs=[
            pl.BlockSpec((bn, tt, D_MODEL), lambda i: (0, i, 0)),
            _const_spec((bn, LRU_WIDTH)),
            _const_spec((hist, LRU_WIDTH)),
            _const_spec((1, D_MODEL)),
            _const_spec((1, D_MODEL)),
            _const_spec((D_MODEL, LRU_WIDTH)),
            _const_spec((D_MODEL, LRU_WIDTH)),
            _const_spec((CONV_WIDTH, LRU_WIDTH)),
            _const_spec((1, LRU_WIDTH)),
            _const_spec((N_LRU_BLOCKS, LRU_BLOCK, 2 * LRU_BLOCK)),
            _const_spec((1, LRU_WIDTH)),
            _const_spec((1, LRU_WIDTH)),
            _const_spec((1, LRU_WIDTH)),
            _const_spec((LRU_WIDTH, D_MODEL)),
        ],
        out_specs=[
            pl.BlockSpec((bn, tt, D_MODEL), lambda i: (0, i, 0)),
            _const_spec((bn, LRU_WIDTH)),
            _const_spec((hist, LRU_WIDTH)),
        ],
        out_shape=[
            jax.ShapeDtypeStruct((bn, t_len, D_MODEL), f32),
            jax.ShapeDtypeStruct((bn, LRU_WIDTH), f32),
            jax.ShapeDtypeStruct((hist, LRU_WIDTH), f32),
        ],
        scratch_shapes=[
            pltpu.VMEM((rows, D_MODEL), f32),
            pltpu.VMEM((rows + hist, LRU_WIDTH), f32),
            pltpu.VMEM((rows, LRU_WIDTH), f32),
            pltpu.VMEM((rows, LRU_WIDTH), bf16),
            pltpu.VMEM((bn, LRU_WIDTH), f32),
        ],
        compiler_params=pltpu.CompilerParams(
            dimension_semantics=("arbitrary",), vmem_limit_bytes=VMEM_LIMIT),
        name="layer_a",
    )(x, h0, conv0_tm, p["g_pre0"], p["g_post0"], p["w_in_x"], p["w_in_g"], p["conv_w"],
      p["conv_b"], p["w_gates"], p["b_r"], p["b_i"], p["lam"], p["a_w_out"])


def _proj_kernel(h_ref, gkv_ref, gpre_ref, wkv_ref, wqg_ref,
                 k_ref, v_ref, kb_ref, vb_ref, q_ref, gate_ref):
    h = h_ref[...]
    hn = h * lax.rsqrt(jnp.mean(h * h, axis=-1, keepdims=True) + EPS)
    kv = jnp.dot((hn * gkv_ref[...]).astype(bf16), wkv_ref[...], preferred_element_type=f32)
    k = kv[:, :N_HEADS * KEY_DIM]
    v = kv[:, N_HEADS * KEY_DIM:]
    n_tok = h.shape[0]
    for hd in range(N_HEADS):
        k_ref[pl.ds(hd, n_tok, stride=N_HEADS), :] = k[:, hd * KEY_DIM:(hd + 1) * KEY_DIM]
        v_ref[pl.ds(hd, n_tok, stride=N_HEADS), :] = v[:, hd * VAL_DIM:(hd + 1) * VAL_DIM]
    kb_ref[...] = k.astype(bf16)
    vb_ref[...] = v.astype(bf16)
    qg = jnp.dot((hn * gpre_ref[...]).astype(bf16), wqg_ref[...], preferred_element_type=f32)
    q_ref[...] = (qg[:, :N_HEADS * KEY_DIM] * QK_SCALE).astype(bf16)
    gate_ref[...] = qg[:, N_HEADS * KEY_DIM:]


def _proj(h2d, p):
    n_rows = h2d.shape[0]
    tm = PROJ_ROWS
    row_spec = pl.BlockSpec((tm, D_MODEL), lambda i: (i, 0))
    head_row_spec = pl.BlockSpec((tm * N_HEADS, KEY_DIM), lambda i: (i, 0))
    return pl.pallas_call(
        _proj_kernel,
        grid=(n_rows // tm,),
        in_specs=[
            row_spec,
            _const_spec((1, D_MODEL)),
            _const_spec((1, D_MODEL)),
            _const_spec((D_MODEL, 2 * D_MODEL)),
            _const_spec((D_MODEL, 2 * D_MODEL)),
        ],
        out_specs=[head_row_spec] * 2 + [row_spec] * 4,
        out_shape=[
            jax.ShapeDtypeStruct((n_rows * N_HEADS, KEY_DIM), f32),
            jax.ShapeDtypeStruct((n_rows * N_HEADS, VAL_DIM), f32),
            jax.ShapeDtypeStruct((n_rows, D_MODEL), bf16),
            jax.ShapeDtypeStruct((n_rows, D_MODEL), bf16),
            jax.ShapeDtypeStruct((n_rows, D_MODEL), bf16),
            jax.ShapeDtypeStruct((n_rows, D_MODEL), f32),
        ],
        compiler_params=pltpu.CompilerParams(
            dimension_semantics=("arbitrary",), vmem_limit_bytes=VMEM_LIMIT),
        name="proj",
    )(h2d, p["g_kv"], p["g_pre1"], p["w_kv"], p["b_w_in"])


def _lambda_full(lamv_ref):
    lv = lamv_ref[...]
    e1 = jnp.exp(jnp.sum(lv[0:1] * lv[1:2], axis=-1, keepdims=True))
    e2 = jnp.exp(jnp.sum(lv[2:3] * lv[3:4], axis=-1, keepdims=True))
    return e1 - e2 + LAM_INIT


def _subln(o, subln):
    return (o * lax.rsqrt(jnp.mean(o * o, axis=-1, keepdims=True) + EPS) * subln) * (1.0 - LAM_INIT)


def _online_softmax_step(st, c, m_s, l_s, acc_s, value_fn):
    m_s[...], l_s[...], acc_s[...] = _softmax_update(
        st, c, (m_s[...], l_s[...], acc_s[...]), value_fn)


def _softmax_update(st, c, state, value_fn):
    tile_max = jnp.max(st, axis=0, keepdims=True) + c
    if state is None:
        p = jnp.exp2(st - (tile_max - c))
        return tile_max, jnp.sum(p, axis=0, keepdims=True), value_fn(p)
    m_old, l_old, acc_old = state
    m_new = jnp.maximum(m_old, tile_max)
    p = jnp.exp2(st - (m_new - c))
    corr = jnp.exp2(m_old - m_new)
    return (m_new, l_old * corr + jnp.sum(p, axis=0, keepdims=True),
            acc_old * corr + value_fn(p))


def _attn_prompt_kernel(slopes_ref, q_ref, k_ref, v_ref, lamv_ref, subln_ref, *rest, tq, n_tiles):
    o_ref = rest[-1]
    head = pl.program_id(1)
    slope = slopes_ref[head]
    cols = MAPS * tq
    qi = n_tiles - 1

    vt = v_ref[...].T
    qt = q_ref[...].T
    feat = lax.broadcasted_iota(jnp.int32, (KEY_DIM, tq), 0)
    zero = jnp.zeros_like(qt)
    q2t = jnp.concatenate([jnp.where(feat < HEAD_DIM, qt, zero),
                           jnp.where(feat >= HEAD_DIM, qt, zero)], axis=1)

    key_off = lax.broadcasted_iota(jnp.int32, (tq, VAL_DIM), 0).astype(f32)
    key_bias = jnp.concatenate([slope * key_off] * (cols // VAL_DIM), axis=1)
    causal = (lax.broadcasted_iota(jnp.int32, (tq, tq), 1)
              >= lax.broadcasted_iota(jnp.int32, (tq, tq), 0))

    def scores(j):
        return jnp.dot(k_ref[j * tq:(j + 1) * tq, :], q2t, preferred_element_type=f32) + key_bias

    state = None
    st_next = scores(0)
    for j in range(n_tiles):
        st = st_next
        if j + 1 < n_tiles:
            st_next = scores(j + 1)
        if j == qi:
            st = jnp.concatenate([jnp.where(causal, st[:, m * tq:(m + 1) * tq], NEG_INF)
                                  for m in range(MAPS)], axis=1)
        c = slope * float((j - qi) * tq)
        state = _softmax_update(
            st, c, state,
            lambda p, j=j: jnp.dot(vt[:, j * tq:(j + 1) * tq], p.astype(bf16),
                                   preferred_element_type=f32))

    _, l, acc = state
    lam = _lambda_full(lamv_ref)
    accn = acc / l
    ot = accn[:, :tq] - lam * accn[:, tq:]
    o_ref[...] = _subln(ot.T, subln_ref[...])


def _attn_prompt(q, kb, vb, p):
    b, t_len, _ = q.shape
    tq = ATTN_TQ
    out = None
    for qi in range(t_len // tq):
        n_keys = (qi + 1) * tq
        in_specs = [
            pl.BlockSpec(memory_space=pltpu.SMEM),
            pl.BlockSpec((None, tq, KEY_DIM), lambda bi, h, qi=qi: (bi, qi, h)),
            pl.BlockSpec((None, n_keys, KEY_DIM), lambda bi, h: (bi, 0, h)),
            pl.BlockSpec((None, n_keys, VAL_DIM), lambda bi, h: (bi, 0, h)),
            _const_spec((4, HEAD_DIM)),
            _const_spec((1, VAL_DIM)),
        ]
        args = [p["slopes"], q, kb, vb, p["lamv"], p["subln"]]
        aliases = {}
        if out is not None:
            in_specs.append(pl.BlockSpec(memory_space=pl.ANY))
            args.append(out)
            aliases = {len(args) - 1: 0}
        out = pl.pallas_call(
            functools.partial(_attn_prompt_kernel, tq=tq, n_tiles=qi + 1),
            grid=(b, N_HEADS),
            in_specs=in_specs,
            out_specs=pl.BlockSpec((None, tq, VAL_DIM), lambda bi, h, qi=qi: (bi, qi, h)),
            out_shape=jax.ShapeDtypeStruct((b, t_len, ATTN_WIDTH), f32),
            input_output_aliases=aliases,
            compiler_params=pltpu.CompilerParams(
                dimension_semantics=("arbitrary", "arbitrary"), vmem_limit_bytes=VMEM_LIMIT),
            name=f"attn_prompt_q{qi}",
        )(*args)
    return out


def _attn_decode_kernel(pt_ref, qt_ref, knew_ref, vnew_ref, slope_ref, posbias_ref, newbias_ref,
                        lamv_ref, subln_ref, *rest, past_len):
    npg = PAGES_PER_STEP
    k_refs = rest[:npg]
    v_refs = rest[npg:2 * npg]
    o_ref = rest[2 * npg]
    qbd_s, m_s, l_s, acc_s = rest[2 * npg + 1:]
    i = pl.program_id(1)

    slope = slope_ref[...]
    col = lax.broadcasted_iota(jnp.int32, (1, DEC_COLS), 1)
    tok = jnp.bitwise_and(col, DEC_TOKENS - 1)

    @pl.when(i == 0)
    def _():
        qt = qt_ref[...]
        col_head = lax.broadcasted_iota(jnp.int32, (KEY_DIM, DEC_COLS), 1) // COLS_PER_HEAD
        for h in range(N_HEADS):
            qbd_s[h * KEY_DIM:(h + 1) * KEY_DIM, :] = jnp.where(col_head == h, qt, jnp.zeros_like(qt))
        m_s[...] = jnp.full((1, DEC_COLS), NEG_INF, f32)
        l_s[...] = jnp.zeros((1, DEC_COLS), f32)
        acc_s[...] = jnp.zeros((VAL_DIM, DEC_COLS), f32)
        st = jnp.dot(knew_ref[...].astype(bf16), qt, preferred_element_type=f32) + newbias_ref[...]
        vb = vnew_ref[...].astype(bf16)
        _online_softmax_step(
            st, jnp.zeros((1, DEC_COLS), f32), m_s, l_s, acc_s,
            lambda p: lax.dot_general(vb, p.astype(bf16), (((0,), (0,)), ((), ())),
                                      preferred_element_type=f32))

    def head_rows(ref, h):
        return ref[pl.ds(h, PAGE_SIZE, stride=N_HEADS), :].astype(bf16)

    c = -slope * (past_len + tok - i * (npg * PAGE_SIZE)).astype(f32)

    for g in range(0, npg, PAGES_PER_TILE):
        kg, vg = k_refs[g:g + PAGES_PER_TILE], v_refs[g:g + PAGES_PER_TILE]
        kcat = jnp.concatenate(
            [jnp.concatenate([head_rows(k_ref, h) for h in range(N_HEADS)], axis=1) for k_ref in kg],
            axis=0)
        st = (jnp.dot(kcat, qbd_s[...], preferred_element_type=f32)
              + posbias_ref[g * PAGE_SIZE:(g + PAGES_PER_TILE) * PAGE_SIZE, :])

        def values(p, vg=vg):
            pt = p.T.astype(bf16)
            blocks = []
            for h in range(0, N_HEADS, 2):
                vpair = jnp.concatenate(
                    [jnp.concatenate([head_rows(v_ref, hh) for v_ref in vg], axis=0)
                     for hh in (h, h + 1)], axis=1)
                prod = jnp.dot(pt[h * COLS_PER_HEAD:(h + 2) * COLS_PER_HEAD, :], vpair,
                               preferred_element_type=f32)
                blocks.append(prod[:COLS_PER_HEAD, :VAL_DIM])
                blocks.append(prod[COLS_PER_HEAD:, VAL_DIM:])
            return jnp.concatenate(blocks, axis=0).T

        _online_softmax_step(st, c, m_s, l_s, acc_s, values)

    @pl.when(i == pl.num_programs(1) - 1)
    def _():
        lam = _lambda_full(lamv_ref)
        accn = (acc_s[...] / l_s[...]).T
        for h in range(N_HEADS):
            r0 = h * COLS_PER_HEAD
            o = accn[r0:r0 + DEC_TOKENS] - lam * accn[r0 + DEC_TOKENS:r0 + 2 * DEC_TOKENS]
            o_ref[:, h * VAL_DIM:(h + 1) * VAL_DIM] = _subln(o, subln_ref[...])


def _decode_bias_tables(slopes):
    col_head = jnp.arange(DEC_COLS) // COLS_PER_HEAD
    col_tok = jnp.arange(DEC_COLS) % DEC_TOKENS
    col_slope = slopes[col_head]
    page = col_slope[None, :] * jnp.arange(PAGES_PER_STEP * PAGE_SIZE, dtype=f32)[:, None]
    row = jnp.arange(NEW_ROWS)
    row_tok = (row // N_HEADS)[:, None]
    ok = ((row % N_HEADS)[:, None] == col_head[None, :]) & (row_tok <= col_tok[None, :])
    new = jnp.where(ok, -col_slope[None, :] * (col_tok[None, :] - row_tok).astype(f32), NEG_INF)
    return page.astype(f32), new.astype(f32), col_slope[None, :]


def _attn_decode(q, k_rows, v_rows, cache_k, cache_v, page_table, p):
    b, n_tok, _ = q.shape
    assert n_tok == DEC_TOKENS
    n_pages = page_table.shape[1]
    npg = PAGES_PER_STEP

    q5 = q.reshape(b, n_tok, N_HEADS, MAPS, HEAD_DIM).transpose(0, 3, 4, 2, 1)
    eye_m = jnp.eye(MAPS, dtype=q.dtype)
    qt = q5[:, :, :, :, None, :] * eye_m[None, :, None, None, :, None]
    qt = qt.reshape(b, KEY_DIM, DEC_COLS)

    n_pool = cache_k.shape[0]
    ck = cache_k.reshape(n_pool * PAGE_ROWS, KEY_DIM)
    cv = cache_v.reshape(n_pool * PAGE_ROWS, VAL_DIM)

    def page_spec(n):
        return pl.BlockSpec((PAGE_ROWS, KEY_DIM), lambda bi, i, pt: (pt[bi, i * npg + n], 0))

    def const_spec(shape):
        return pl.BlockSpec(shape, lambda bi, i, pt: (0, 0))

    grid_spec = pltpu.PrefetchScalarGridSpec(
        num_scalar_prefetch=1,
        grid=(b, n_pages // npg),
        in_specs=[
            pl.BlockSpec((None, KEY_DIM, DEC_COLS), lambda bi, i, pt: (bi, 0, 0)),
            pl.BlockSpec((NEW_ROWS, KEY_DIM), lambda bi, i, pt: (bi, 0)),
            pl.BlockSpec((NEW_ROWS, VAL_DIM), lambda bi, i, pt: (bi, 0)),
            const_spec((1, DEC_COLS)),
            const_spec((npg * PAGE_SIZE, DEC_COLS)),
            const_spec((NEW_ROWS, DEC_COLS)),
            const_spec((4, HEAD_DIM)),
            const_spec((1, VAL_DIM)),
        ] + [page_spec(n) for n in range(npg)] + [page_spec(n) for n in range(npg)],
        out_specs=pl.BlockSpec((None, n_tok, ATTN_WIDTH), lambda bi, i, pt: (bi, 0, 0)),
        scratch_shapes=[
            pltpu.VMEM((N_HEADS * KEY_DIM, DEC_COLS), bf16),
            pltpu.VMEM((1, DEC_COLS), f32),
            pltpu.VMEM((1, DEC_COLS), f32),
            pltpu.VMEM((VAL_DIM, DEC_COLS), f32),
        ],
    )
    kern = functools.partial(_attn_decode_kernel, past_len=n_pages * PAGE_SIZE)
    return pl.pallas_call(
        kern,
        grid_spec=grid_spec,
        out_shape=jax.ShapeDtypeStruct((b, n_tok, ATTN_WIDTH), f32),
        compiler_params=pltpu.CompilerParams(
            dimension_semantics=("arbitrary", "arbitrary"), vmem_limit_bytes=VMEM_LIMIT),
        name="attn_decode",
    )(page_table, qt, k_rows, v_rows, p["slope_cols"], p["page_bias"], p["new_bias"], p["lamv"],
      p["subln"], *([ck] * npg), *([cv] * npg))


def _tail_kernel(o_ref, gate_ref, h_ref, wout_ref, gpost_ref, y_ref):
    z = (o_ref[...] * jax.nn.silu(gate_ref[...])).astype(bf16)
    out = jnp.dot(z, wout_ref[...], preferred_element_type=f32)
    y_ref[...] = h_ref[...] + _rms(out, gpost_ref[...])


def _tail(o2d, gate2d, h2d, p):
    n_rows = o2d.shape[0]
    tm = PROJ_ROWS
    row_spec = pl.BlockSpec((tm, D_MODEL), lambda i: (i, 0))
    return pl.pallas_call(
        _tail_kernel,
        grid=(n_rows // tm,),
        in_specs=[row_spec, row_spec, row_spec,
                  _const_spec((ATTN_WIDTH, D_MODEL)), _const_spec((1, D_MODEL))],
        out_specs=row_spec,
        out_shape=jax.ShapeDtypeStruct((n_rows, D_MODEL), f32),
        compiler_params=pltpu.CompilerParams(
            dimension_semantics=("arbitrary",), vmem_limit_bytes=VMEM_LIMIT),
        name="tail",
    )(o2d, gate2d, h2d, p["b_w_out"], p["g_post1"])


def _trunk(x, h0, conv0, p, attend):
    b, t_len, _ = x.shape
    conv0_tm = conv0.transpose(1, 0, 2).reshape((CONV_WIDTH - 1) * b, LRU_WIDTH)
    h1, h_last, conv_last = _layer_a(x, h0, conv0_tm, p)
    h2d = h1.reshape(b * t_len, D_MODEL)
    k, v, kb, vb, q, gate = _proj(h2d, p)
    o = attend(q.reshape(b, t_len, -1), k, v, kb.reshape(b, t_len, -1), vb.reshape(b, t_len, -1))
    y = _tail(o.reshape(b * t_len, ATTN_WIDTH), gate, h2d, p)
    new_conv = conv_last.reshape(CONV_WIDTH - 1, b, LRU_WIDTH).transpose(1, 0, 2)
    return (y.reshape(b, t_len, D_MODEL), k.reshape(b, t_len, N_HEADS, KEY_DIM),
            v.reshape(b, t_len, N_HEADS, VAL_DIM), h_last[None], new_conv[None])


def kernel(x_prompt, x_sample, cache_k, cache_v, page_table, state_lru_h, state_conv, norm_pre, norm_post, a_w_in, a_conv_w, a_conv_b, a_w_r, a_b_r, a_w_i, a_b_i, a_lambda, a_w_out, kv_norm, w_kv, b_w_in, b_lambda_q1, b_lambda_k1, b_lambda_q2, b_lambda_k2, b_subln, b_w_out):
    slopes = jnp.asarray([2.0 ** (-8.0 * (h + 1) / N_HEADS) for h in range(N_HEADS)], dtype=f32) * LOG2E
    page_bias, new_bias, slope_cols = _decode_bias_tables(slopes)
    p = dict(
        g_pre0=norm_pre[0:1], g_post0=norm_post[0:1], g_pre1=norm_pre[1:2], g_post1=norm_post[1:2],
        w_in_x=a_w_in[0, :, :LRU_WIDTH].astype(bf16), w_in_g=a_w_in[0, :, LRU_WIDTH:].astype(bf16),
        conv_w=a_conv_w[0], conv_b=a_conv_b[0:1],
        w_gates=jnp.concatenate([a_w_r[0], a_w_i[0]], axis=-1).astype(bf16),
        b_r=a_b_r[0:1], b_i=a_b_i[0:1], lam=a_lambda[0:1], a_w_out=a_w_out[0].astype(bf16),
        g_kv=kv_norm[None, :], w_kv=w_kv.astype(bf16), b_w_in=b_w_in[0].astype(bf16),
        lamv=jnp.concatenate([b_lambda_q1, b_lambda_k1, b_lambda_q2, b_lambda_k2], axis=0),
        subln=b_subln[0:1], b_w_out=b_w_out[0].astype(bf16),
        slopes=slopes, slope_cols=slope_cols, page_bias=page_bias, new_bias=new_bias,
    )

    def attend_prompt(q, k_rows, v_rows, kb, vb):
        return _attn_prompt(q, kb, vb, p)

    def attend_sample(q, k_rows, v_rows, kb, vb):
        return _attn_decode(q, k_rows, v_rows, cache_k, cache_v, page_table, p)

    bp = x_prompt.shape[0]
    h0_prompt = jnp.zeros((bp, LRU_WIDTH), f32)
    conv0_prompt = jnp.zeros((bp, CONV_WIDTH - 1, LRU_WIDTH), f32)
    y_p, k_p, v_p, lru_p, conv_p = _trunk(x_prompt, h0_prompt, conv0_prompt, p, attend_prompt)
    y_s, k_s, v_s, lru_s, conv_s = _trunk(x_sample, state_lru_h[0], state_conv[0], p, attend_sample)
    return (y_p, y_s, k_p, v_p, lru_p, conv_p, k_s, v_s, lru_s, conv_s)
```

```python
import functools
import math

import jax
import jax.numpy as jnp
from jax import lax
from jax.experimental import pallas as pl
from jax.experimental.pallas import tpu as pltpu

f32 = jnp.float32
bf16 = jnp.bfloat16

D_MODEL = 1024
LRU_WIDTH = D_MODEL
N_LRU_BLOCKS = 8
LRU_BLOCK = LRU_WIDTH // N_LRU_BLOCKS
CONV_WIDTH = 4
LRU_C = 8.0
N_HEADS = 8
HEAD_DIM = D_MODEL // (2 * N_HEADS)
KEY_DIM = 2 * HEAD_DIM
VAL_DIM = 2 * HEAD_DIM
ATTN_WIDTH = N_HEADS * VAL_DIM
PAGE_SIZE = 128
EPS = 1e-6
NEG_INF = -1e30
LAM_INIT = 0.8 - 0.6 * math.exp(-0.3 * 1)
LOG2E = math.log2(math.e)
QK_SCALE = HEAD_DIM ** -0.5 * LOG2E

VMEM_LIMIT = 48 * 1024 * 1024
FUSED_VMEM_LIMIT = 56 * 1024 * 1024

LAYER_A_ROWS = 256
PROJ_ROWS = 256
ATTN_TQ = 512
PAGES_PER_STEP = 16
PAGES_PER_TILE = 16
MAPS = 2
DEC_TOKENS = 8
COLS_PER_HEAD = MAPS * DEC_TOKENS
DEC_COLS = N_HEADS * COLS_PER_HEAD
PAGE_ROWS = PAGE_SIZE * N_HEADS
NEW_ROWS = DEC_TOKENS * N_HEADS


def _const_spec(shape):
    n = len(shape)
    return pl.BlockSpec(shape, lambda *_: (0,) * n)


def _rms(x, g):
    ms = jnp.mean(x * x, axis=-1, keepdims=True)
    return x * lax.rsqrt(ms + EPS) * g


def _layer_a_kernel(x_ref, h0_ref, conv0_ref, gpre_ref, gpost_ref, winx_ref, wing_ref,
                    cw_ref, cb_ref, wg_ref, br_ref, bi_ref, lam_ref, wout_ref,
                    h1_ref, hlast_ref, convlast_ref,
                    x_s, ext_s, gate_s, y_s, hc_s, *, bn, tt):
    i = pl.program_id(0)
    rows = bn * tt
    hist = (CONV_WIDTH - 1) * bn

    @pl.when(i == 0)
    def _():
        ext_s[0:hist, :] = conv0_ref[...]
        hc_s[...] = h0_ref[...]

    x = jnp.swapaxes(x_ref[...], 0, 1).reshape(rows, D_MODEL)
    x_s[...] = x
    xn = _rms(x, gpre_ref[...]).astype(bf16)
    ext_s[hist:hist + rows, :] = jnp.dot(xn, winx_ref[...], preferred_element_type=f32)
    gate_s[...] = jnp.dot(xn, wing_ref[...], preferred_element_type=f32)

    for j in range(N_LRU_BLOCKS):
        sl = slice(j * LRU_BLOCK, (j + 1) * LRU_BLOCK)
        xc = cb_ref[:, sl]
        for k in range(CONV_WIDTH):
            xc = xc + ext_s[k * bn:k * bn + rows, sl] * cw_ref[k:k + 1, sl]
        g = jnp.dot(xc.astype(bf16), wg_ref[j], preferred_element_type=f32)
        r = jax.nn.sigmoid(g[:, :LRU_BLOCK] + br_ref[:, sl])
        gi = jax.nn.sigmoid(g[:, LRU_BLOCK:] + bi_ref[:, sl])
        log_a = -LRU_C * r * jax.nn.softplus(-lam_ref[:, sl])
        a = jnp.exp(log_a)
        mult = jnp.sqrt(jnp.tanh(-log_a) * (1.0 + a * a))
        bterm = mult * (gi * xc)
        h = hc_s[:, sl]
        hs = []
        for t in range(tt):
            h = a[t * bn:(t + 1) * bn] * h + bterm[t * bn:(t + 1) * bn]
            hs.append(h)
        hc_s[:, sl] = h
        hseq = jnp.concatenate(hs, axis=0)
        y_s[:, sl] = (hseq * jax.nn.silu(gate_s[:, sl])).astype(bf16)

    out = jnp.dot(y_s[...], wout_ref[...], preferred_element_type=f32)
    h1 = x_s[...] + _rms(out, gpost_ref[...])
    h1_ref[...] = jnp.swapaxes(h1.reshape(tt, bn, D_MODEL), 0, 1)
    hlast_ref[...] = hc_s[...]
    convlast_ref[...] = ext_s[rows:rows + hist, :]
    ext_s[0:hist, :] = ext_s[rows:rows + hist, :]


def _layer_a(x, h0, conv0_tm, p):
    bn, t_len, _ = x.shape
    tt = LAYER_A_ROWS // bn
    rows = bn * tt
    hist = (CONV_WIDTH - 1) * bn
    grid = (t_len // tt,)
    kern = functools.partial(_layer_a_kernel, bn=bn, tt=tt)
    return pl.pallas_call(
        kern,
        grid=grid,
        in_specs=[
            pl.BlockSpec((bn, tt, D_MODEL), lambda i: (0, i, 0)),
            _const_spec((bn, LRU_WIDTH)),
            _const_spec((hist, LRU_WIDTH)),
            _const_spec((1, D_MODEL)),
            _const_spec((1, D_MODEL)),
            _const_spec((D_MODEL, LRU_WIDTH)),
            _const_spec((D_MODEL, LRU_WIDTH)),
            _const_spec((CONV_WIDTH, LRU_WIDTH)),
            _const_spec((1, LRU_WIDTH)),
            _const_spec((N_LRU_BLOCKS, LRU_BLOCK, 2 * LRU_BLOCK)),
            _const_spec((1, LRU_WIDTH)),
            _const_spec((1, LRU_WIDTH)),
            _const_spec((1, LRU_WIDTH)),
            _const_spec((LRU_WIDTH, D_MODEL)),
        ],
        out_specs=[
            pl.BlockSpec((bn, tt, D_MODEL), lambda i: (0, i, 0)),
            _const_spec((bn, LRU_WIDTH)),
            _const_spec((hist, LRU_WIDTH)),
        ],
        out_shape=[
            jax.ShapeDtypeStruct((bn, t_len, D_MODEL), f32),
            jax.ShapeDtypeStruct((bn, LRU_WIDTH), f32),
            jax.ShapeDtypeStruct((hist, LRU_WIDTH), f32),
        ],
        scratch_shapes=[
            pltpu.VMEM((rows, D_MODEL), f32),
            pltpu.VMEM((rows + hist, LRU_WIDTH), f32),
            pltpu.VMEM((rows, LRU_WIDTH), f32),
            pltpu.VMEM((rows, LRU_WIDTH), bf16),
            pltpu.VMEM((bn, LRU_WIDTH), f32),
        ],
        compiler_params=pltpu.CompilerParams(
            dimension_semantics=("arbitrary",), vmem_limit_bytes=VMEM_LIMIT),
        name="layer_a",
    )(x, h0, conv0_tm, p["g_pre0"], p["g_post0"], p["w_in_x"], p["w_in_g"], p["conv_w"],
      p["conv_b"], p["w_gates"], p["b_r"], p["b_i"], p["lam"], p["a_w_out"])


def _proj_kernel(h_ref, gkv_ref, gpre_ref, wkv_ref, wqg_ref,
                 k_ref, v_ref, kb_ref, vb_ref, q_ref, gate_ref):
    h = h_ref[...]
    hn = h * lax.rsqrt(jnp.mean(h * h, axis=-1, keepdims=True) + EPS)
    kv = jnp.dot((hn * gkv_ref[...]).astype(bf16), wkv_ref[...], preferred_element_type=f32)
    k = kv[:, :N_HEADS * KEY_DIM]
    v = kv[:, N_HEADS * KEY_DIM:]
    n_tok = h.shape[0]
    for hd in range(N_HEADS):
        k_ref[pl.ds(hd, n_tok, stride=N_HEADS), :] = k[:, hd * KEY_DIM:(hd + 1) * KEY_DIM]
        v_ref[pl.ds(hd, n_tok, stride=N_HEADS), :] = v[:, hd * VAL_DIM:(hd + 1) * VAL_DIM]
    kb_ref[...] = k.astype(bf16)
    vb_ref[...] = v.astype(bf16)
    qg = jnp.dot((hn * gpre_ref[...]).astype(bf16), wqg_ref[...], preferred_element_type=f32)
    q_ref[...] = (qg[:, :N_HEADS * KEY_DIM] * QK_SCALE).astype(bf16)
    gate_ref[...] = qg[:, N_HEADS * KEY_DIM:]


def _proj(h2d, p):
    n_rows = h2d.shape[0]
    tm = PROJ_ROWS
    row_spec = pl.BlockSpec((tm, D_MODEL), lambda i: (i, 0))
    head_row_spec = pl.BlockSpec((tm * N_HEADS, KEY_DIM), lambda i: (i, 0))
    return pl.pallas_call(
        _proj_kernel,
        grid=(n_rows // tm,),
        in_specs=[
            row_spec,
            _const_spec((1, D_MODEL)),
            _const_spec((1, D_MODEL)),
            _const_spec((D_MODEL, 2 * D_MODEL)),
            _const_spec((D_MODEL, 2 * D_MODEL)),
        ],
        out_specs=[head_row_spec] * 2 + [row_spec] * 4,
        out_shape=[
            jax.ShapeDtypeStruct((n_rows * N_HEADS, KEY_DIM), f32),
            jax.ShapeDtypeStruct((n_rows * N_HEADS, VAL_DIM), f32),
            jax.ShapeDtypeStruct((n_rows, D_MODEL), bf16),
            jax.ShapeDtypeStruct((n_rows, D_MODEL), bf16),
            jax.ShapeDtypeStruct((n_rows, D_MODEL), bf16),
            jax.ShapeDtypeStruct((n_rows, D_MODEL), f32),
        ],
        compiler_params=pltpu.CompilerParams(
            dimension_semantics=("arbitrary",), vmem_limit_bytes=VMEM_LIMIT),
        name="proj",
    )(h2d, p["g_kv"], p["g_pre1"], p["w_kv"], p["b_w_in"])


def _lambda_full(lamv_ref):
    lv = lamv_ref[...]
    e1 = jnp.exp(jnp.sum(lv[0:1] * lv[1:2], axis=-1, keepdims=True))
    e2 = jnp.exp(jnp.sum(lv[2:3] * lv[3:4], axis=-1, keepdims=True))
    return e1 - e2 + LAM_INIT


def _subln(o, subln):
    return (o * lax.rsqrt(jnp.mean(o * o, axis=-1, keepdims=True) + EPS) * subln) * (1.0 - LAM_INIT)


def _online_softmax_step(st, c, m_s, l_s, acc_s, value_fn):
    m_s[...], l_s[...], acc_s[...] = _softmax_update(
        st, c, (m_s[...], l_s[...], acc_s[...]), value_fn)


def _softmax_update(st, c, state, value_fn):
    tile_max = jnp.max(st, axis=0, keepdims=True) + c
    if state is None:
        p = jnp.exp2(st - (tile_max - c))
        return tile_max, jnp.sum(p, axis=0, keepdims=True), value_fn(p)
    m_old, l_old, acc_old = state
    m_new = jnp.maximum(m_old, tile_max)
    p = jnp.exp2(st - (m_new - c))
    corr = jnp.exp2(m_old - m_new)
    return (m_new, l_old * corr + jnp.sum(p, axis=0, keepdims=True),
            acc_old * corr + value_fn(p))


def _prompt_unit(slope, q_ref, k_ref, v_ref, lamv_ref, subln_ref, o_ref, *, tq, n_tiles):
    cols = MAPS * tq
    qi = n_tiles - 1

    vt = v_ref[0:n_tiles * tq, :].T
    qt = q_ref[...].T
    feat = lax.broadcasted_iota(jnp.int32, (KEY_DIM, tq), 0)
    zero = jnp.zeros_like(qt)
    q2t = jnp.concatenate([jnp.where(feat < HEAD_DIM, qt, zero),
                           jnp.where(feat >= HEAD_DIM, qt, zero)], axis=1)

    key_off = lax.broadcasted_iota(jnp.int32, (tq, VAL_DIM), 0).astype(f32)
    key_bias = jnp.concatenate([slope * key_off] * (cols // VAL_DIM), axis=1)
    causal = (lax.broadcasted_iota(jnp.int32, (tq, tq), 1)
              >= lax.broadcasted_iota(jnp.int32, (tq, tq), 0))

    def scores(j):
        return jnp.dot(k_ref[j * tq:(j + 1) * tq, :], q2t, preferred_element_type=f32) + key_bias

    state = None
    st_next = scores(0)
    for j in range(n_tiles):
        st = st_next
        if j + 1 < n_tiles:
            st_next = scores(j + 1)
        if j == qi:
            st = jnp.concatenate([jnp.where(causal, st[:, m * tq:(m + 1) * tq], NEG_INF)
                                  for m in range(MAPS)], axis=1)
        c = slope * float((j - qi) * tq)
        state = _softmax_update(
            st, c, state,
            lambda p, j=j: jnp.dot(vt[:, j * tq:(j + 1) * tq], p.astype(bf16),
                                   preferred_element_type=f32))

    _, l, acc = state
    lam = _lambda_full(lamv_ref)
    accn = acc / l
    ot = accn[:, :tq] - lam * accn[:, tq:]
    o_ref[...] = _subln(ot.T, subln_ref[...])


def _decode_step(i, n_steps, qt_ref, knew_ref, vnew_ref, slope_ref, posbias_ref, newbias_ref,
                 lamv_ref, subln_ref, k_refs, v_refs, o_ref, qbd_s, m_s, l_s, acc_s, *, past_len):
    npg = PAGES_PER_STEP

    slope = slope_ref[...]
    col = lax.broadcasted_iota(jnp.int32, (1, DEC_COLS), 1)
    tok = jnp.bitwise_and(col, DEC_TOKENS - 1)

    @pl.when(i == 0)
    def _():
        qt = qt_ref[...]
        col_head = lax.broadcasted_iota(jnp.int32, (KEY_DIM, DEC_COLS), 1) // COLS_PER_HEAD
        for h in range(N_HEADS):
            qbd_s[h * KEY_DIM:(h + 1) * KEY_DIM, :] = jnp.where(col_head == h, qt, jnp.zeros_like(qt))
        m_s[...] = jnp.full((1, DEC_COLS), NEG_INF, f32)
        l_s[...] = jnp.zeros((1, DEC_COLS), f32)
        acc_s[...] = jnp.zeros((VAL_DIM, DEC_COLS), f32)
        st = jnp.dot(knew_ref[...].astype(bf16), qt, preferred_element_type=f32) + newbias_ref[...]
        vb = vnew_ref[...].astype(bf16)
        _online_softmax_step(
            st, jnp.zeros((1, DEC_COLS), f32), m_s, l_s, acc_s,
            lambda p: lax.dot_general(vb, p.astype(bf16), (((0,), (0,)), ((), ())),
                                      preferred_element_type=f32))

    def head_rows(ref, h):
        return ref[pl.ds(h, PAGE_SIZE, stride=N_HEADS), :].astype(bf16)

    c = -slope * (past_len + tok - i * (npg * PAGE_SIZE)).astype(f32)

    for g in range(0, npg, PAGES_PER_TILE):
        kg, vg = k_refs[g:g + PAGES_PER_TILE], v_refs[g:g + PAGES_PER_TILE]
        kcat = jnp.concatenate(
            [jnp.concatenate([head_rows(k_ref, h) for h in range(N_HEADS)], axis=1) for k_ref in kg],
            axis=0)
        st = (jnp.dot(kcat, qbd_s[...], preferred_element_type=f32)
              + posbias_ref[g * PAGE_SIZE:(g + PAGES_PER_TILE) * PAGE_SIZE, :])

        def values(p, vg=vg):
            pt = p.T.astype(bf16)
            blocks = []
            for h in range(0, N_HEADS, 2):
                vpair = jnp.concatenate(
                    [jnp.concatenate([head_rows(v_ref, hh) for v_ref in vg], axis=0)
                     for hh in (h, h + 1)], axis=1)
                prod = jnp.dot(pt[h * COLS_PER_HEAD:(h + 2) * COLS_PER_HEAD, :], vpair,
                               preferred_element_type=f32)
                blocks.append(prod[:COLS_PER_HEAD, :VAL_DIM])
                blocks.append(prod[COLS_PER_HEAD:, VAL_DIM:])
            return jnp.concatenate(blocks, axis=0).T

        _online_softmax_step(st, c, m_s, l_s, acc_s, values)

    @pl.when(i == n_steps - 1)
    def _():
        lam = _lambda_full(lamv_ref)
        accn = (acc_s[...] / l_s[...]).T
        for h in range(N_HEADS):
            r0 = h * COLS_PER_HEAD
            o = accn[r0:r0 + DEC_TOKENS] - lam * accn[r0 + DEC_TOKENS:r0 + 2 * DEC_TOKENS]
            o_ref[:, h * VAL_DIM:(h + 1) * VAL_DIM] = _subln(o, subln_ref[...])


def _decode_bias_tables(slopes):
    col_head = jnp.arange(DEC_COLS) // COLS_PER_HEAD
    col_tok = jnp.arange(DEC_COLS) % DEC_TOKENS
    col_slope = slopes[col_head]
    page = col_slope[None, :] * jnp.arange(PAGES_PER_STEP * PAGE_SIZE, dtype=f32)[:, None]
    row = jnp.arange(NEW_ROWS)
    row_tok = (row // N_HEADS)[:, None]
    ok = ((row % N_HEADS)[:, None] == col_head[None, :]) & (row_tok <= col_tok[None, :])
    new = jnp.where(ok, -col_slope[None, :] * (col_tok[None, :] - row_tok).astype(f32), NEG_INF)
    return page.astype(f32), new.astype(f32), col_slope[None, :]


def _attn_fused_kernel(pt_ref, qt_ref, knew_ref, vnew_ref, slope_ref, posbias_ref, newbias_ref,
                       lamv_ref, subln_ref, *rest, past_len, steps_per_seq, tq, n_qtiles):
    npg = PAGES_PER_STEP
    k_refs = rest[:npg]
    v_refs = rest[npg:2 * npg]
    slopes_ref, q_ref, kp_ref, vp_ref, o_dec_ref, o_prompt_ref, qbd_s, m_s, l_s, acc_s = rest[2 * npg:]
    s = pl.program_id(0)

    _decode_step(lax.rem(s, steps_per_seq), steps_per_seq, qt_ref, knew_ref, vnew_ref, slope_ref,
                 posbias_ref, newbias_ref, lamv_ref, subln_ref, k_refs, v_refs, o_dec_ref,
                 qbd_s, m_s, l_s, acc_s, past_len=past_len)

    qi = lax.rem(s, n_qtiles)
    slope = slopes_ref[lax.rem(lax.div(s, n_qtiles), N_HEADS)]
    for t in range(n_qtiles):
        @pl.when(qi == t)
        def _(t=t):
            _prompt_unit(slope, q_ref, kp_ref, vp_ref, lamv_ref, subln_ref, o_prompt_ref,
                         tq=tq, n_tiles=t + 1)


def _attn_fused(q_p, kb_p, vb_p, q, k_rows, v_rows, cache_k, cache_v, page_table, p):
    b, n_tok, _ = q.shape
    assert n_tok == DEC_TOKENS
    n_pages = page_table.shape[1]
    npg = PAGES_PER_STEP
    steps_per_seq = n_pages // npg
    n_steps = b * steps_per_seq
    bp, t_len, _ = q_p.shape
    tq = ATTN_TQ
    n_qtiles = t_len // tq
    assert n_steps == bp * N_HEADS * n_qtiles

    q5 = q.reshape(b, n_tok, N_HEADS, MAPS, HEAD_DIM).transpose(0, 3, 4, 2, 1)
    eye_m = jnp.eye(MAPS, dtype=q.dtype)
    qt = q5[:, :, :, :, None, :] * eye_m[None, :, None, None, :, None]
    qt = qt.reshape(b, KEY_DIM, DEC_COLS)

    n_pool = cache_k.shape[0]
    ck = cache_k.reshape(n_pool * PAGE_ROWS, KEY_DIM)
    cv = cache_v.reshape(n_pool * PAGE_ROWS, VAL_DIM)

    def seq(s):
        return s // steps_per_seq

    def page_spec(n):
        return pl.BlockSpec(
            (PAGE_ROWS, KEY_DIM),
            lambda s, pt: (pt[s // steps_per_seq, (s % steps_per_seq) * npg + n], 0))

    def const_spec(shape):
        return pl.BlockSpec(shape, lambda s, pt: (0, 0))

    def prompt_idx(s):
        return s // (n_qtiles * N_HEADS), (s // n_qtiles) % N_HEADS, s % n_qtiles

    def q_map(s, pt):
        bi, h, qi = prompt_idx(s)
        return bi, qi, h

    def kv_map(s, pt):
        bi, h, _ = prompt_idx(s)
        return bi, 0, h

    grid_spec = pltpu.PrefetchScalarGridSpec(
        num_scalar_prefetch=1,
        grid=(n_steps,),
        in_specs=[
            pl.BlockSpec((None, KEY_DIM, DEC_COLS), lambda s, pt: (seq(s), 0, 0)),
            pl.BlockSpec((NEW_ROWS, KEY_DIM), lambda s, pt: (seq(s), 0)),
            pl.BlockSpec((NEW_ROWS, VAL_DIM), lambda s, pt: (seq(s), 0)),
            const_spec((1, DEC_COLS)),
            const_spec((npg * PAGE_SIZE, DEC_COLS)),
            const_spec((NEW_ROWS, DEC_COLS)),
            const_spec((4, HEAD_DIM)),
            const_spec((1, VAL_DIM)),
        ] + [page_spec(n) for n in range(npg)] + [page_spec(n) for n in range(npg)] + [
            pl.BlockSpec(memory_space=pltpu.SMEM),
            pl.BlockSpec((None, tq, KEY_DIM), q_map),
            pl.BlockSpec((None, t_len, KEY_DIM), kv_map),
            pl.BlockSpec((None, t_len, VAL_DIM), kv_map),
        ],
        out_specs=[
            pl.BlockSpec((None, n_tok, ATTN_WIDTH), lambda s, pt: (seq(s), 0, 0)),
            pl.BlockSpec((None, tq, VAL_DIM), q_map),
        ],
        scratch_shapes=[
            pltpu.VMEM((N_HEADS * KEY_DIM, DEC_COLS), bf16),
            pltpu.VMEM((1, DEC_COLS), f32),
            pltpu.VMEM((1, DEC_COLS), f32),
            pltpu.VMEM((VAL_DIM, DEC_COLS), f32),
        ],
    )
    kern = functools.partial(_attn_fused_kernel, past_len=n_pages * PAGE_SIZE,
                             steps_per_seq=steps_per_seq, tq=tq, n_qtiles=n_qtiles)
    return pl.pallas_call(
        kern,
        grid_spec=grid_spec,
        out_shape=[jax.ShapeDtypeStruct((b, n_tok, ATTN_WIDTH), f32),
                   jax.ShapeDtypeStruct((bp, t_len, ATTN_WIDTH), f32)],
        compiler_params=pltpu.CompilerParams(
            dimension_semantics=("arbitrary",), vmem_limit_bytes=FUSED_VMEM_LIMIT),
        name="attn_fused",
    )(page_table, qt, k_rows, v_rows, p["slope_cols"], p["page_bias"], p["new_bias"], p["lamv"],
      p["subln"], *([ck] * npg), *([cv] * npg), p["slopes"], q_p, kb_p, vb_p)


def _tail_kernel(o_ref, gate_ref, h_ref, wout_ref, gpost_ref, y_ref):
    z = (o_ref[...] * jax.nn.silu(gate_ref[...])).astype(bf16)
    out = jnp.dot(z, wout_ref[...], preferred_element_type=f32)
    y_ref[...] = h_ref[...] + _rms(out, gpost_ref[...])


def _tail(o2d, gate2d, h2d, p):
    n_rows = o2d.shape[0]
    tm = PROJ_ROWS
    row_spec = pl.BlockSpec((tm, D_MODEL), lambda i: (i, 0))
    return pl.pallas_call(
        _tail_kernel,
        grid=(n_rows // tm,),
        in_specs=[row_spec, row_spec, row_spec,
                  _const_spec((ATTN_WIDTH, D_MODEL)), _const_spec((1, D_MODEL))],
        out_specs=row_spec,
        out_shape=jax.ShapeDtypeStruct((n_rows, D_MODEL), f32),
        compiler_params=pltpu.CompilerParams(
            dimension_semantics=("arbitrary",), vmem_limit_bytes=VMEM_LIMIT),
        name="tail",
    )(o2d, gate2d, h2d, p["b_w_out"], p["g_post1"])


def _trunk_head(x, h0, conv0, p):
    b, t_len, _ = x.shape
    conv0_tm = conv0.transpose(1, 0, 2).reshape((CONV_WIDTH - 1) * b, LRU_WIDTH)
    h1, h_last, conv_last = _layer_a(x, h0, conv0_tm, p)
    h2d = h1.reshape(b * t_len, D_MODEL)
    k, v, kb, vb, q, gate = _proj(h2d, p)
    new_conv = conv_last.reshape(CONV_WIDTH - 1, b, LRU_WIDTH).transpose(1, 0, 2)
    return dict(h2d=h2d, gate=gate, k_rows=k, v_rows=v, q=q.reshape(b, t_len, -1),
                kb=kb.reshape(b, t_len, -1), vb=vb.reshape(b, t_len, -1),
                k=k.reshape(b, t_len, N_HEADS, KEY_DIM), v=v.reshape(b, t_len, N_HEADS, VAL_DIM),
                lru=h_last[None], conv=new_conv[None])


def _trunk_tail(t, o, p):
    b, t_len, _ = o.shape
    y = _tail(o.reshape(b * t_len, ATTN_WIDTH), t["gate"], t["h2d"], p)
    return y.reshape(b, t_len, D_MODEL)


def kernel(x_prompt, x_sample, cache_k, cache_v, page_table, state_lru_h, state_conv, norm_pre, norm_post, a_w_in, a_conv_w, a_conv_b, a_w_r, a_b_r, a_w_i, a_b_i, a_lambda, a_w_out, kv_norm, w_kv, b_w_in, b_lambda_q1, b_lambda_k1, b_lambda_q2, b_lambda_k2, b_subln, b_w_out):
    slopes = jnp.asarray([2.0 ** (-8.0 * (h + 1) / N_HEADS) for h in range(N_HEADS)], dtype=f32) * LOG2E
    page_bias, new_bias, slope_cols = _decode_bias_tables(slopes)
    p = dict(
        g_pre0=norm_pre[0:1], g_post0=norm_post[0:1], g_pre1=norm_pre[1:2], g_post1=norm_post[1:2],
        w_in_x=a_w_in[0, :, :LRU_WIDTH].astype(bf16), w_in_g=a_w_in[0, :, LRU_WIDTH:].astype(bf16),
        conv_w=a_conv_w[0], conv_b=a_conv_b[0:1],
        w_gates=jnp.concatenate([a_w_r[0], a_w_i[0]], axis=-1).astype(bf16),
        b_r=a_b_r[0:1], b_i=a_b_i[0:1], lam=a_lambda[0:1], a_w_out=a_w_out[0].astype(bf16),
        g_kv=kv_norm[None, :], w_kv=w_kv.astype(bf16), b_w_in=b_w_in[0].astype(bf16),
        lamv=jnp.concatenate([b_lambda_q1, b_lambda_k1, b_lambda_q2, b_lambda_k2], axis=0),
        subln=b_subln[0:1], b_w_out=b_w_out[0].astype(bf16),
        slopes=slopes, slope_cols=slope_cols, page_bias=page_bias, new_bias=new_bias,
    )

    bp = x_prompt.shape[0]
    h0_prompt = jnp.zeros((bp, LRU_WIDTH), f32)
    conv0_prompt = jnp.zeros((bp, CONV_WIDTH - 1, LRU_WIDTH), f32)
    tp = _trunk_head(x_prompt, h0_prompt, conv0_prompt, p)
    ts = _trunk_head(x_sample, state_lru_h[0], state_conv[0], p)
    o_s, o_p = _attn_fused(tp["q"], tp["kb"], tp["vb"], ts["q"], ts["k_rows"], ts["v_rows"],
                           cache_k, cache_v, page_table, p)
    y_p = _trunk_tail(tp, o_p, p)
    y_s = _trunk_tail(ts, o_s, p)
    return (y_p, y_s, tp["k"], tp["v"], tp["lru"], tp["conv"], ts["k"], ts["v"], ts["lru"], ts["conv"])
```

```python
import functools
import math

import jax
import jax.numpy as jnp
from jax import lax
from jax.experimental import pallas as pl
from jax.experimental.pallas import tpu as pltpu

f32 = jnp.float32
bf16 = jnp.bfloat16

D_MODEL = 1024
LRU_WIDTH = D_MODEL
N_LRU_BLOCKS = 8
LRU_BLOCK = LRU_WIDTH // N_LRU_BLOCKS
CONV_WIDTH = 4
LRU_C = 8.0
N_HEADS = 8
HEAD_DIM = D_MODEL // (2 * N_HEADS)
KEY_DIM = 2 * HEAD_DIM
VAL_DIM = 2 * HEAD_DIM
ATTN_WIDTH = N_HEADS * VAL_DIM
PAGE_SIZE = 128
EPS = 1e-6
NEG_INF = -1e30
LAM_INIT = 0.8 - 0.6 * math.exp(-0.3 * 1)
LOG2E = math.log2(math.e)
QK_SCALE = HEAD_DIM ** -0.5 * LOG2E

VMEM_LIMIT = 48 * 1024 * 1024
FUSED_VMEM_LIMIT = 56 * 1024 * 1024

LAYER_A_ROWS = 256
PROJ_ROWS = 256
ATTN_TQ = 512
PAGES_PER_STEP = 16
PAGES_PER_TILE = 4
MAPS = 2
DEC_TOKENS = 8
COLS_PER_HEAD = MAPS * DEC_TOKENS
DEC_COLS = N_HEADS * COLS_PER_HEAD
PAGE_ROWS = PAGE_SIZE * N_HEADS
NEW_ROWS = DEC_TOKENS * N_HEADS


def _const_spec(shape):
    n = len(shape)
    return pl.BlockSpec(shape, lambda *_: (0,) * n)


def _rms(x, g):
    ms = jnp.mean(x * x, axis=-1, keepdims=True)
    return x * lax.rsqrt(ms + EPS) * g


def _layer_a_kernel(x_ref, h0_ref, conv0_ref, gpre_ref, gpost_ref, winx_ref, wing_ref,
                    cw_ref, cb_ref, wg_ref, br_ref, bi_ref, lam_ref, wout_ref,
                    h1_ref, hlast_ref, convlast_ref,
                    x_s, ext_s, gate_s, y_s, hc_s, *, bn, tt):
    i = pl.program_id(0)
    rows = bn * tt
    hist = (CONV_WIDTH - 1) * bn

    @pl.when(i == 0)
    def _():
        ext_s[0:hist, :] = conv0_ref[...]
        hc_s[...] = h0_ref[...]

    x = jnp.swapaxes(x_ref[...], 0, 1).reshape(rows, D_MODEL)
    x_s[...] = x
    xn = _rms(x, gpre_ref[...]).astype(bf16)
    ext_s[hist:hist + rows, :] = jnp.dot(xn, winx_ref[...], preferred_element_type=f32)
    gate_s[...] = jnp.dot(xn, wing_ref[...], preferred_element_type=f32)

    for j in range(N_LRU_BLOCKS):
        sl = slice(j * LRU_BLOCK, (j + 1) * LRU_BLOCK)
        xc = cb_ref[:, sl]
        for k in range(CONV_WIDTH):
            xc = xc + ext_s[k * bn:k * bn + rows, sl] * cw_ref[k:k + 1, sl]
        g = jnp.dot(xc.astype(bf16), wg_ref[j], preferred_element_type=f32)
        r = jax.nn.sigmoid(g[:, :LRU_BLOCK] + br_ref[:, sl])
        gi = jax.nn.sigmoid(g[:, LRU_BLOCK:] + bi_ref[:, sl])
        log_a = -LRU_C * r * jax.nn.softplus(-lam_ref[:, sl])
        a = jnp.exp(log_a)
        z = jnp.tanh(-log_a) * (1.0 + a * a)
        mult = jnp.where(z > 0.0, z * lax.rsqrt(z), 0.0)
        bterm = mult * (gi * xc)
        h = hc_s[:, sl]
        hs = []
        for t in range(tt):
            h = a[t * bn:(t + 1) * bn] * h + bterm[t * bn:(t + 1) * bn]
            hs.append(h)
        hc_s[:, sl] = h
        hseq = jnp.concatenate(hs, axis=0)
        y_s[:, sl] = (hseq * jax.nn.silu(gate_s[:, sl])).astype(bf16)

    out = jnp.dot(y_s[...], wout_ref[...], preferred_element_type=f32)
    h1 = x_s[...] + _rms(out, gpost_ref[...])
    h1_ref[...] = jnp.swapaxes(h1.reshape(tt, bn, D_MODEL), 0, 1)
    hlast_ref[...] = hc_s[...]
    convlast_ref[...] = ext_s[rows:rows + hist, :]
    ext_s[0:hist, :] = ext_s[rows:rows + hist, :]


def _layer_a(x, h0, conv0_tm, p):
    bn, t_len, _ = x.shape
    tt = LAYER_A_ROWS // bn
    rows = bn * tt
    hist = (CONV_WIDTH - 1) * bn
    grid = (t_len // tt,)
    kern = functools.partial(_layer_a_kernel, bn=bn, tt=tt)
    return pl.pallas_call(
        kern,
        grid=grid,
        in_specs=[
            pl.BlockSpec((bn, tt, D_MODEL), lambda i: (0, i, 0)),
            _const_spec((bn, LRU_WIDTH)),
            _const_spec((hist, LRU_WIDTH)),
            _const_spec((1, D_MODEL)),
            _const_spec((1, D_MODEL)),
            _const_spec((D_MODEL, LRU_WIDTH)),
            _const_spec((D_MODEL, LRU_WIDTH)),
            _const_spec((CONV_WIDTH, LRU_WIDTH)),
            _const_spec((1, LRU_WIDTH)),
            _const_spec((N_LRU_BLOCKS, LRU_BLOCK, 2 * LRU_BLOCK)),
            _const_spec((1, LRU_WIDTH)),
            _const_spec((1, LRU_WIDTH)),
            _const_spec((1, LRU_WIDTH)),
            _const_spec((LRU_WIDTH, D_MODEL)),
        ],
        out_specs=[
            pl.BlockSpec((bn, tt, D_MODEL), lambda i: (0, i, 0)),
            _const_spec((bn, LRU_WIDTH)),
            _const_spec((hist, LRU_WIDTH)),
        ],
        out_shape=[
            jax.ShapeDtypeStruct((bn, t_len, D_MODEL), f32),
            jax.ShapeDtypeStruct((bn, LRU_WIDTH), f32),
            jax.ShapeDtypeStruct((hist, LRU_WIDTH), f32),
        ],
        scratch_shapes=[
            pltpu.VMEM((rows, D_MODEL), f32),
            pltpu.VMEM((rows + hist, LRU_WIDTH), f32),
            pltpu.VMEM((rows, LRU_WIDTH), f32),
            pltpu.VMEM((rows, LRU_WIDTH), bf16),
            pltpu.VMEM((bn, LRU_WIDTH), f32),
        ],
        compiler_params=pltpu.CompilerParams(
            dimension_semantics=("arbitrary",), vmem_limit_bytes=VMEM_LIMIT),
        name="layer_a",
    )(x, h0, conv0_tm, p["g_pre0"], p["g_post0"], p["w_in_x"], p["w_in_g"], p["conv_w"],
      p["conv_b"], p["w_gates"], p["b_r"], p["b_i"], p["lam"], p["a_w_out"])


def _proj_kernel(h_ref, gkv_ref, gpre_ref, wkv_ref, wqg_ref,
                 k_ref, v_ref, kb_ref, vb_ref, q_ref, gate_ref):
    h = h_ref[...]
    hn = h * lax.rsqrt(jnp.mean(h * h, axis=-1, keepdims=True) + EPS)
    kv = jnp.dot((hn * gkv_ref[...]).astype(bf16), wkv_ref[...], preferred_element_type=f32)
    k = kv[:, :N_HEADS * KEY_DIM]
    v = kv[:, N_HEADS * KEY_DIM:]
    n_tok = h.shape[0]
    for hd in range(N_HEADS):
        k_ref[pl.ds(hd, n_tok, stride=N_HEADS), :] = k[:, hd * KEY_DIM:(hd + 1) * KEY_DIM]
        v_ref[pl.ds(hd, n_tok, stride=N_HEADS), :] = v[:, hd * VAL_DIM:(hd + 1) * VAL_DIM]
    kb_ref[...] = k.astype(bf16)
    vb_ref[...] = v.astype(bf16)
    qg = jnp.dot((hn * gpre_ref[...]).astype(bf16), wqg_ref[...], preferred_element_type=f32)
    q_ref[...] = (qg[:, :N_HEADS * KEY_DIM] * QK_SCALE).astype(bf16)
    gate_ref[...] = qg[:, N_HEADS * KEY_DIM:]


def _proj(h2d, p):
    n_rows = h2d.shape[0]
    tm = PROJ_ROWS
    row_spec = pl.BlockSpec((tm, D_MODEL), lambda i: (i, 0))
    head_row_spec = pl.BlockSpec((tm * N_HEADS, KEY_DIM), lambda i: (i, 0))
    return pl.pallas_call(
        _proj_kernel,
        grid=(n_rows // tm,),
        in_specs=[
            row_spec,
            _const_spec((1, D_MODEL)),
            _const_spec((1, D_MODEL)),
            _const_spec((D_MODEL, 2 * D_MODEL)),
            _const_spec((D_MODEL, 2 * D_MODEL)),
        ],
        out_specs=[head_row_spec] * 2 + [row_spec] * 4,
        out_shape=[
            jax.ShapeDtypeStruct((n_rows * N_HEADS, KEY_DIM), f32),
            jax.ShapeDtypeStruct((n_rows * N_HEADS, VAL_DIM), f32),
            jax.ShapeDtypeStruct((n_rows, D_MODEL), bf16),
            jax.ShapeDtypeStruct((n_rows, D_MODEL), bf16),
            jax.ShapeDtypeStruct((n_rows, D_MODEL), bf16),
            jax.ShapeDtypeStruct((n_rows, D_MODEL), f32),
        ],
        compiler_params=pltpu.CompilerParams(
            dimension_semantics=("arbitrary",), vmem_limit_bytes=VMEM_LIMIT),
        name="proj",
    )(h2d, p["g_kv"], p["g_pre1"], p["w_kv"], p["b_w_in"])


def _lambda_full(lamv_ref):
    lv = lamv_ref[...]
    e1 = jnp.exp(jnp.sum(lv[0:1] * lv[1:2], axis=-1, keepdims=True))
    e2 = jnp.exp(jnp.sum(lv[2:3] * lv[3:4], axis=-1, keepdims=True))
    return e1 - e2 + LAM_INIT


def _subln(o, subln):
    return (o * lax.rsqrt(jnp.mean(o * o, axis=-1, keepdims=True) + EPS) * subln) * (1.0 - LAM_INIT)


def _online_softmax_step(st, c, m_s, l_s, acc_s, value_fn):
    m_s[...], l_s[...], acc_s[...] = _softmax_update(
        st, c, (m_s[...], l_s[...], acc_s[...]), value_fn)


def _softmax_update(st, c, state, value_fn):
    tile_max = jnp.max(st, axis=0, keepdims=True) + c
    if state is None:
        p = jnp.exp2(st - (tile_max - c))
        return tile_max, jnp.sum(p, axis=0, keepdims=True), value_fn(p)
    m_old, l_old, acc_old = state
    m_new = jnp.maximum(m_old, tile_max)
    p = jnp.exp2(st - (m_new - c))
    corr = jnp.exp2(m_old - m_new)
    return (m_new, l_old * corr + jnp.sum(p, axis=0, keepdims=True),
            acc_old * corr + value_fn(p))


def _prompt_unit(slope, q_ref, k_ref, v_ref, lamv_ref, subln_ref, o_ref, *, tq, n_tiles):
    cols = MAPS * tq
    qi = n_tiles - 1

    vt = v_ref[0:n_tiles * tq, :].T
    qt = q_ref[...].T
    feat = lax.broadcasted_iota(jnp.int32, (KEY_DIM, tq), 0)
    zero = jnp.zeros_like(qt)
    q2t = jnp.concatenate([jnp.where(feat < HEAD_DIM, qt, zero),
                           jnp.where(feat >= HEAD_DIM, qt, zero)], axis=1)

    key_off = lax.broadcasted_iota(jnp.int32, (tq, VAL_DIM), 0).astype(f32)
    key_bias = jnp.concatenate([slope * key_off] * (cols // VAL_DIM), axis=1)
    causal = (lax.broadcasted_iota(jnp.int32, (tq, tq), 1)
              >= lax.broadcasted_iota(jnp.int32, (tq, tq), 0))

    def scores(j):
        return jnp.dot(k_ref[j * tq:(j + 1) * tq, :], q2t, preferred_element_type=f32) + key_bias

    state = None
    st_next = scores(0)
    for j in range(n_tiles):
        st = st_next
        if j + 1 < n_tiles:
            st_next = scores(j + 1)
        if j == qi:
            st = jnp.concatenate([jnp.where(causal, st[:, m * tq:(m + 1) * tq], NEG_INF)
                                  for m in range(MAPS)], axis=1)
        c = slope * float((j - qi) * tq)
        state = _softmax_update(
            st, c, state,
            lambda p, j=j: jnp.dot(vt[:, j * tq:(j + 1) * tq], p.astype(bf16),
                                   preferred_element_type=f32))

    _, l, acc = state
    lam = _lambda_full(lamv_ref)
    accn = acc / l
    ot = accn[:, :tq] - lam * accn[:, tq:]
    o_ref[...] = _subln(ot.T, subln_ref[...])


def _decode_step(i, n_steps, qt_ref, knew_ref, vnew_ref, slope_ref, posbias_ref, newbias_ref,
                 lamv_ref, subln_ref, k_refs, v_refs, o_ref, qbd_s, m_s, l_s, acc_s, *, past_len):
    npg = PAGES_PER_STEP

    slope = slope_ref[...]
    col = lax.broadcasted_iota(jnp.int32, (1, DEC_COLS), 1)
    tok = jnp.bitwise_and(col, DEC_TOKENS - 1)

    @pl.when(i == 0)
    def _():
        qt = qt_ref[...]
        col_head = lax.broadcasted_iota(jnp.int32, (KEY_DIM, DEC_COLS), 1) // COLS_PER_HEAD
        for h in range(N_HEADS):
            qbd_s[h * KEY_DIM:(h + 1) * KEY_DIM, :] = jnp.where(col_head == h, qt, jnp.zeros_like(qt))
        m_s[...] = jnp.full((1, DEC_COLS), NEG_INF, f32)
        l_s[...] = jnp.zeros((1, DEC_COLS), f32)
        acc_s[...] = jnp.zeros((VAL_DIM, DEC_COLS), f32)
        st = jnp.dot(knew_ref[...].astype(bf16), qt, preferred_element_type=f32) + newbias_ref[...]
        vb = vnew_ref[...].astype(bf16)
        _online_softmax_step(
            st, jnp.zeros((1, DEC_COLS), f32), m_s, l_s, acc_s,
            lambda p: lax.dot_general(vb, p.astype(bf16), (((0,), (0,)), ((), ())),
                                      preferred_element_type=f32))

    def head_rows(ref, h):
        return ref[pl.ds(h, PAGE_SIZE, stride=N_HEADS), :].astype(bf16)

    c = -slope * (past_len + tok - i * (npg * PAGE_SIZE)).astype(f32)

    def scores(g):
        kcat = jnp.concatenate(
            [jnp.concatenate([head_rows(k_ref, h) for h in range(N_HEADS)], axis=1)
             for k_ref in k_refs[g:g + PAGES_PER_TILE]], axis=0)
        return (jnp.dot(kcat, qbd_s[...], preferred_element_type=f32)
                + posbias_ref[g * PAGE_SIZE:(g + PAGES_PER_TILE) * PAGE_SIZE, :])

    tiles = range(0, npg, PAGES_PER_TILE)
    all_scores = [scores(g) for g in tiles]
    for g, st in zip(tiles, all_scores):
        vg = v_refs[g:g + PAGES_PER_TILE]

        def values(p, vg=vg):
            pt = p.T.astype(bf16)
            blocks = []
            for h in range(0, N_HEADS, 2):
                vpair = jnp.concatenate(
                    [jnp.concatenate([head_rows(v_ref, hh) for v_ref in vg], axis=0)
                     for hh in (h, h + 1)], axis=1)
                prod = jnp.dot(pt[h * COLS_PER_HEAD:(h + 2) * COLS_PER_HEAD, :], vpair,
                               preferred_element_type=f32)
                blocks.append(prod[:COLS_PER_HEAD, :VAL_DIM])
                blocks.append(prod[COLS_PER_HEAD:, VAL_DIM:])
            return jnp.concatenate(blocks, axis=0).T

        _online_softmax_step(st, c, m_s, l_s, acc_s, values)

    @pl.when(i == n_steps - 1)
    def _():
        lam = _lambda_full(lamv_ref)
        accn = (acc_s[...] / l_s[...]).T
        for h in range(N_HEADS):
            r0 = h * COLS_PER_HEAD
            o = accn[r0:r0 + DEC_TOKENS] - lam * accn[r0 + DEC_TOKENS:r0 + 2 * DEC_TOKENS]
            o_ref[:, h * VAL_DIM:(h + 1) * VAL_DIM] = _subln(o, subln_ref[...])


def _decode_bias_tables(slopes):
    col_head = jnp.arange(DEC_COLS) // COLS_PER_HEAD
    col_tok = jnp.arange(DEC_COLS) % DEC_TOKENS
    col_slope = slopes[col_head]
    page = col_slope[None, :] * jnp.arange(PAGES_PER_STEP * PAGE_SIZE, dtype=f32)[:, None]
    row = jnp.arange(NEW_ROWS)
    row_tok = (row // N_HEADS)[:, None]
    ok = ((row % N_HEADS)[:, None] == col_head[None, :]) & (row_tok <= col_tok[None, :])
    new = jnp.where(ok, -col_slope[None, :] * (col_tok[None, :] - row_tok).astype(f32), NEG_INF)
    return page.astype(f32), new.astype(f32), col_slope[None, :]


def _attn_fused_kernel(pt_ref, qt_ref, knew_ref, vnew_ref, slope_ref, posbias_ref, newbias_ref,
                       lamv_ref, subln_ref, *rest, past_len, steps_per_seq, tq, n_qtiles):
    npg = PAGES_PER_STEP
    k_refs = rest[:npg]
    v_refs = rest[npg:2 * npg]
    slopes_ref, q_ref, kp_ref, vp_ref, o_dec_ref, o_prompt_ref, qbd_s, m_s, l_s, acc_s = rest[2 * npg:]
    s = pl.program_id(0)

    _decode_step(lax.rem(s, steps_per_seq), steps_per_seq, qt_ref, knew_ref, vnew_ref, slope_ref,
                 posbias_ref, newbias_ref, lamv_ref, subln_ref, k_refs, v_refs, o_dec_ref,
                 qbd_s, m_s, l_s, acc_s, past_len=past_len)

    qi = lax.rem(s, n_qtiles)
    slope = slopes_ref[lax.rem(lax.div(s, n_qtiles), N_HEADS)]
    for t in range(n_qtiles):
        @pl.when(qi == t)
        def _(t=t):
            _prompt_unit(slope, q_ref, kp_ref, vp_ref, lamv_ref, subln_ref, o_prompt_ref,
                         tq=tq, n_tiles=t + 1)


def _attn_fused(q_p, kb_p, vb_p, q, k_rows, v_rows, cache_k, cache_v, page_table, p):
    b, n_tok, _ = q.shape
    assert n_tok == DEC_TOKENS
    n_pages = page_table.shape[1]
    npg = PAGES_PER_STEP
    steps_per_seq = n_pages // npg
    n_steps = b * steps_per_seq
    bp, t_len, _ = q_p.shape
    tq = ATTN_TQ
    n_qtiles = t_len // tq
    assert n_steps == bp * N_HEADS * n_qtiles

    q5 = q.reshape(b, n_tok, N_HEADS, MAPS, HEAD_DIM).transpose(0, 3, 4, 2, 1)
    eye_m = jnp.eye(MAPS, dtype=q.dtype)
    qt = q5[:, :, :, :, None, :] * eye_m[None, :, None, None, :, None]
    qt = qt.reshape(b, KEY_DIM, DEC_COLS)

    n_pool = cache_k.shape[0]
    ck = cache_k.reshape(n_pool * PAGE_ROWS, KEY_DIM)
    cv = cache_v.reshape(n_pool * PAGE_ROWS, VAL_DIM)

    def seq(s):
        return s // steps_per_seq

    pages_flat = page_table.reshape(b * n_pages)

    def page_spec(n):
        return pl.BlockSpec((PAGE_ROWS, KEY_DIM), lambda s, pt: (pt[s * npg + n], 0))

    def const_spec(shape):
        return pl.BlockSpec(shape, lambda s, pt: (0, 0))

    def prompt_idx(s):
        return s // (n_qtiles * N_HEADS), (s // n_qtiles) % N_HEADS, s % n_qtiles

    def q_map(s, pt):
        bi, h, qi = prompt_idx(s)
        return bi, qi, h

    def kv_map(s, pt):
        bi, h, _ = prompt_idx(s)
        return bi, 0, h

    grid_spec = pltpu.PrefetchScalarGridSpec(
        num_scalar_prefetch=1,
        grid=(n_steps,),
        in_specs=[
            pl.BlockSpec((None, KEY_DIM, DEC_COLS), lambda s, pt: (seq(s), 0, 0)),
            pl.BlockSpec((NEW_ROWS, KEY_DIM), lambda s, pt: (seq(s), 0)),
            pl.BlockSpec((NEW_ROWS, VAL_DIM), lambda s, pt: (seq(s), 0)),
            const_spec((1, DEC_COLS)),
            const_spec((npg * PAGE_SIZE, DEC_COLS)),
            const_spec((NEW_ROWS, DEC_COLS)),
            const_spec((4, HEAD_DIM)),
            const_spec((1, VAL_DIM)),
        ] + [page_spec(n) for n in range(npg)] + [page_spec(n) for n in range(npg)] + [
            pl.BlockSpec(memory_space=pltpu.SMEM),
            pl.BlockSpec((None, tq, KEY_DIM), q_map),
            pl.BlockSpec((None, t_len, KEY_DIM), kv_map),
            pl.BlockSpec((None, t_len, VAL_DIM), kv_map),
        ],
        out_specs=[
            pl.BlockSpec((None, n_tok, ATTN_WIDTH), lambda s, pt: (seq(s), 0, 0)),
            pl.BlockSpec((None, tq, VAL_DIM), q_map),
        ],
        scratch_shapes=[
            pltpu.VMEM((N_HEADS * KEY_DIM, DEC_COLS), bf16),
            pltpu.VMEM((1, DEC_COLS), f32),
            pltpu.VMEM((1, DEC_COLS), f32),
            pltpu.VMEM((VAL_DIM, DEC_COLS), f32),
        ],
    )
    kern = functools.partial(_attn_fused_kernel, past_len=n_pages * PAGE_SIZE,
                             steps_per_seq=steps_per_seq, tq=tq, n_qtiles=n_qtiles)
    return pl.pallas_call(
        kern,
        grid_spec=grid_spec,
        out_shape=[jax.ShapeDtypeStruct((b, n_tok, ATTN_WIDTH), f32),
                   jax.ShapeDtypeStruct((bp, t_len, ATTN_WIDTH), f32)],
        compiler_params=pltpu.CompilerParams(
            dimension_semantics=("arbitrary",), vmem_limit_bytes=FUSED_VMEM_LIMIT),
        name="attn_fused",
    )(pages_flat, qt, k_rows, v_rows, p["slope_cols"], p["page_bias"], p["new_bias"], p["lamv"],
      p["subln"], *([ck] * npg), *([cv] * npg), p["slopes"], q_p, kb_p, vb_p)


def _tail_kernel(o_ref, gate_ref, h_ref, wout_ref, gpost_ref, y_ref):
    z = (o_ref[...] * jax.nn.silu(gate_ref[...])).astype(bf16)
    out = jnp.dot(z, wout_ref[...], preferred_element_type=f32)
    y_ref[...] = h_ref[...] + _rms(out, gpost_ref[...])


def _tail(o2d, gate2d, h2d, p):
    n_rows = o2d.shape[0]
    tm = PROJ_ROWS
    row_spec = pl.BlockSpec((tm, D_MODEL), lambda i: (i, 0))
    return pl.pallas_call(
        _tail_kernel,
        grid=(n_rows // tm,),
        in_specs=[row_spec, row_spec, row_spec,
                  _const_spec((ATTN_WIDTH, D_MODEL)), _const_spec((1, D_MODEL))],
        out_specs=row_spec,
        out_shape=jax.ShapeDtypeStruct((n_rows, D_MODEL), f32),
        compiler_params=pltpu.CompilerParams(
            dimension_semantics=("arbitrary",), vmem_limit_bytes=VMEM_LIMIT),
        name="tail",
    )(o2d, gate2d, h2d, p["b_w_out"], p["g_post1"])


def _trunk_head(x, h0, conv0, p):
    b, t_len, _ = x.shape
    conv0_tm = conv0.transpose(1, 0, 2).reshape((CONV_WIDTH - 1) * b, LRU_WIDTH)
    h1, h_last, conv_last = _layer_a(x, h0, conv0_tm, p)
    h2d = h1.reshape(b * t_len, D_MODEL)
    k, v, kb, vb, q, gate = _proj(h2d, p)
    new_conv = conv_last.reshape(CONV_WIDTH - 1, b, LRU_WIDTH).transpose(1, 0, 2)
    return dict(h2d=h2d, gate=gate, k_rows=k, v_rows=v, q=q.reshape(b, t_len, -1),
                kb=kb.reshape(b, t_len, -1), vb=vb.reshape(b, t_len, -1),
                k=k.reshape(b, t_len, N_HEADS, KEY_DIM), v=v.reshape(b, t_len, N_HEADS, VAL_DIM),
                lru=h_last[None], conv=new_conv[None])


def _trunk_tail(t, o, p):
    b, t_len, _ = o.shape
    y = _tail(o.reshape(b * t_len, ATTN_WIDTH), t["gate"], t["h2d"], p)
    return y.reshape(b, t_len, D_MODEL)


def kernel(x_prompt, x_sample, cache_k, cache_v, page_table, state_lru_h, state_conv, norm_pre, norm_post, a_w_in, a_conv_w, a_conv_b, a_w_r, a_b_r, a_w_i, a_b_i, a_lambda, a_w_out, kv_norm, w_kv, b_w_in, b_lambda_q1, b_lambda_k1, b_lambda_q2, b_lambda_k2, b_subln, b_w_out):
    slopes = jnp.asarray([2.0 ** (-8.0 * (h + 1) / N_HEADS) for h in range(N_HEADS)], dtype=f32) * LOG2E
    page_bias, new_bias, slope_cols = _decode_bias_tables(slopes)
    p = dict(
        g_pre0=norm_pre[0:1], g_post0=norm_post[0:1], g_pre1=norm_pre[1:2], g_post1=norm_post[1:2],
        w_in_x=a_w_in[0, :, :LRU_WIDTH].astype(bf16), w_in_g=a_w_in[0, :, LRU_WIDTH:].astype(bf16),
        conv_w=a_conv_w[0], conv_b=a_conv_b[0:1],
        w_gates=jnp.concatenate([a_w_r[0], a_w_i[0]], axis=-1).astype(bf16),
        b_r=a_b_r[0:1], b_i=a_b_i[0:1], lam=a_lambda[0:1], a_w_out=a_w_out[0].astype(bf16),
        g_kv=kv_norm[None, :], w_kv=w_kv.astype(bf16), b_w_in=b_w_in[0].astype(bf16),
        lamv=jnp.concatenate([b_lambda_q1, b_lambda_k1, b_lambda_q2, b_lambda_k2], axis=0),
        subln=b_subln[0:1], b_w_out=b_w_out[0].astype(bf16),
        slopes=slopes, slope_cols=slope_cols, page_bias=page_bias, new_bias=new_bias,
    )

    bp = x_prompt.shape[0]
    h0_prompt = jnp.zeros((bp, LRU_WIDTH), f32)
    conv0_prompt = jnp.zeros((bp, CONV_WIDTH - 1, LRU_WIDTH), f32)
    tp = _trunk_head(x_prompt, h0_prompt, conv0_prompt, p)
    ts = _trunk_head(x_sample, state_lru_h[0], state_conv[0], p)
    o_s, o_p = _attn_fused(tp["q"], tp["kb"], tp["vb"], ts["q"], ts["k_rows"], ts["v_rows"],
                           cache_k, cache_v, page_table, p)
    y_p = _trunk_tail(tp, o_p, p)
    y_s = _trunk_tail(ts, o_s, p)
    return (y_p, y_s, tp["k"], tp["v"], tp["lru"], tp["conv"], ts["k"], ts["v"], ts["lru"], ts["conv"])
```
